```python
import math
import jax, jax.numpy as jnp
from jax import lax
import numpy as np

D_MODEL = 1024
BATCH = 2
SEQ = 16384
DEPTH = 2

N_A_LAYERS = DEPTH // 2
N_B_LAYERS = DEPTH - N_A_LAYERS
N_DENSE_FFN = (DEPTH + 1) // 2
N_MOE_FFN = DEPTH // 2

RET_HEADS = 4
RET_QK_DIM = D_MODEL // RET_HEADS
RET_V_DIM = 2 * D_MODEL // RET_HEADS
RET_CHUNK = 128

DIFF_HEADS = 8
DIFF_HEAD_DIM = D_MODEL // DIFF_HEADS // 2
DIFF_V_DIM = 2 * DIFF_HEAD_DIM
Q_BLOCK = 128

D_FF_DENSE = 2816
N_EXPERTS = 8
TOP_K = 2
D_FF_EXPERT = 3584

ROPE_THETA = 10000.0
NORM_EPS = 1e-6
SUBLN_EPS = 1e-5

kernel_name = "yoco_retention_diffattn_moe_trunk"


def rms_norm(x, g, eps=NORM_EPS):
    xf = x.astype(jnp.float32)
    y = (xf * lax.rsqrt(jnp.mean(xf * xf, axis=-1, keepdims=True) + eps)).astype(x.dtype)
    return y if g is None else y * g


def rope_tables(seq, dim, dtype):
    inv = 1.0 / (ROPE_THETA ** (jnp.arange(0, dim, 2, dtype=jnp.float32) / dim))
    ang = jnp.arange(seq, dtype=jnp.float32)[:, None] * inv[None, :]
    ang = jnp.concatenate([ang, ang], axis=-1)
    return jnp.cos(ang).astype(dtype), jnp.sin(ang).astype(dtype)


def apply_rope(x, cos, sin):
    x1, x2 = jnp.split(x, 2, axis=-1)
    return x * cos + jnp.concatenate([-x2, x1], axis=-1) * sin


def retention(h, w_in, w_out):
    B, S, D = h.shape
    H, dk, dv, C = RET_HEADS, RET_QK_DIM, RET_V_DIM, RET_CHUNK
    dt = h.dtype
    proj = h @ w_in
    q, k, v, g = jnp.split(proj, [D, 2 * D, 4 * D], axis=-1)
    q = q.reshape(B, S, H, dk).transpose(0, 2, 1, 3)
    k = k.reshape(B, S, H, dk).transpose(0, 2, 1, 3)
    v = v.reshape(B, S, H, dv).transpose(0, 2, 1, 3)
    cos, sin = rope_tables(S, dk, dt)
    q = apply_rope(q, cos, sin)
    k = apply_rope(k, cos, sin) * (dk ** -0.5)

    log_gamma = jnp.log1p(-jnp.exp2(-5.0 - jnp.arange(H, dtype=jnp.float32)))
    pos = jnp.arange(C, dtype=jnp.float32)
    rel = pos[:, None] - pos[None, :]
    intra_decay = jnp.where(rel >= 0, jnp.exp(log_gamma[:, None, None] * jnp.maximum(rel, 0.0)), 0.0).astype(dt)
    q_decay = jnp.exp(log_gamma[:, None] * (pos + 1.0)).astype(dt)
    k_decay = jnp.exp(log_gamma[:, None] * (C - 1.0 - pos)).astype(dt)
    chunk_decay = jnp.exp(log_gamma * C).astype(dt)

    N = S // C
    qc = q.reshape(B, H, N, C, dk)
    kc = k.reshape(B, H, N, C, dk)
    vc = v.reshape(B, H, N, C, dv)
    scores = jnp.einsum('bhncd,bhnkd->bhnck', qc, kc) * intra_decay[None, :, None]
    intra = jnp.einsum('bhnck,bhnke->bhnce', scores, vc)

    def step(state, xs):
        qi, ki, vi = xs
        inter_i = jnp.einsum('bhcd,bhde->bhce', qi, state) * q_decay[None, :, :, None]
        state = state * chunk_decay[None, :, None, None] + jnp.einsum(
            'bhcd,bhce->bhde', ki * k_decay[None, :, :, None], vi)
        return state, inter_i

    state0 = jnp.zeros((B, H, dk, dv), dt)
    _, inter = lax.scan(step, state0, (qc.transpose(2, 0, 1, 3, 4),
                                        kc.transpose(2, 0, 1, 3, 4),
                                        vc.transpose(2, 0, 1, 3, 4)))
    inter = inter.transpose(1, 2, 0, 3, 4)
    o = (intra + inter).reshape(B, H, S, dv).transpose(0, 2, 1, 3)
    o = rms_norm(o, None).reshape(B, S, H * dv)
    return (jax.nn.silu(g) * o) @ w_out


def shared_kv(x, kv_norm_g, w_kv):
    B, S, _ = x.shape
    H, d = DIFF_HEADS, DIFF_HEAD_DIM
    kv = rms_norm(x, kv_norm_g) @ w_kv
    k, v = jnp.split(kv, [H * 2 * d], axis=-1)
    k = k.reshape(B, S, H, 2, d).transpose(0, 2, 3, 1, 4)
    cos, sin = rope_tables(S, d, x.dtype)
    k = apply_rope(k, cos, sin)
    v = v.reshape(B, S, H, DIFF_V_DIM).transpose(0, 2, 1, 3)
    return k, v


def diff_attention(h, k, v, w_q, lam_q1, lam_k1, lam_q2, lam_k2, subln_g, w_o, lambda_init):
    B, S, _ = h.shape
    H, d, C = DIFF_HEADS, DIFF_HEAD_DIM, Q_BLOCK
    q = (h @ w_q).reshape(B, S, H, 2, d).transpose(0, 2, 3, 1, 4)
    cos, sin = rope_tables(S, d, h.dtype)
    q = apply_rope(q, cos, sin) * (d ** -0.5)
    lam = (jnp.exp(jnp.sum(lam_q1.astype(jnp.float32) * lam_k1.astype(jnp.float32)))
           - jnp.exp(jnp.sum(lam_q2.astype(jnp.float32) * lam_k2.astype(jnp.float32)))
           + lambda_init)
    nq = S // C
    qb = q.reshape(B, H, 2, nq, C, d).transpose(3, 0, 1, 2, 4, 5)
    key_pos = jnp.arange(S)
    neg = jnp.finfo(jnp.float32).min

    def block(args):
        qi, bi = args
        logits = jnp.einsum('bhpcd,bhpsd->bhpcs', qi, k).astype(jnp.float32)
        qpos = bi * C + jnp.arange(C)
        mask = key_pos[None, :] <= qpos[:, None]
        a = jax.nn.softmax(jnp.where(mask, logits, neg), axis=-1)
        attn = a[:, :, 0] - lam * a[:, :, 1]
        return jnp.einsum('bhcs,bhse->bhce', attn.astype(v.dtype), v)

    out = lax.map(block, (qb, jnp.arange(nq)))
    out = out.transpose(1, 0, 3, 2, 4).reshape(B, S, H, DIFF_V_DIM)
    out = rms_norm(out, subln_g, SUBLN_EPS) * (1.0 - lambda_init)
    return out.reshape(B, S, H * DIFF_V_DIM) @ w_o


def swiglu(h, w_gate, w_up, w_down):
    return (jax.nn.silu(h @ w_gate) * (h @ w_up)) @ w_down


def moe_swiglu(h, w_router, w_gate, w_up, w_down):
    B, S, D = h.shape
    t = h.reshape(B * S, D)
    logits = (t @ w_router).astype(jnp.float32)
    top_vals, top_idx = lax.top_k(logits, TOP_K)
    top_w = jax.nn.softmax(top_vals, axis=-1)
    gates = jnp.sum(jax.nn.one_hot(top_idx, N_EXPERTS, dtype=jnp.float32) * top_w[..., None], axis=1)
    gates = gates.astype(t.dtype)
    out = jnp.zeros_like(t)
    for e in range(N_EXPERTS):
        out = out + swiglu(t, w_gate[e], w_up[e], w_down[e]) * gates[:, e:e + 1]
    return out.reshape(B, S, D)


def setup_inputs(seed: int = 0) -> dict:
    key = jax.random.key(seed)
    ks = jax.random.split(key, 24)
    D, f32 = D_MODEL, jnp.float32

    def nrm(k, shape, fan_in, scale=1.0):
        return jax.random.normal(k, shape, f32) * (scale * fan_in ** -0.5)

    out_scale = (2.0 * DEPTH) ** -0.5
    return {
        "x": jax.random.normal(ks[0], (BATCH, SEQ, D), f32),
        "attn_norm_g": 1.0 + 0.02 * jax.random.normal(ks[1], (DEPTH, D), f32),
        "ffn_norm_g": 1.0 + 0.02 * jax.random.normal(ks[2], (DEPTH, D), f32),
        "ret_w_in": nrm(ks[3], (N_A_LAYERS, D, 6 * D), D),
        "ret_w_out": nrm(ks[4], (N_A_LAYERS, 2 * D, D), 2 * D, out_scale),
        "kv_norm_g": 1.0 + 0.02 * jax.random.normal(ks[5], (D,), f32),
        "w_kv": nrm(ks[6], (D, DIFF_HEADS * 2 * DIFF_HEAD_DIM + DIFF_HEADS * DIFF_V_DIM), D),
        "diff_w_q": nrm(ks[7], (N_B_LAYERS, D, DIFF_HEADS * 2 * DIFF_HEAD_DIM), D),
        "diff_lam_q1": 0.1 * jax.random.normal(ks[8], (N_B_LAYERS, DIFF_HEAD_DIM), f32),
        "diff_lam_k1": 0.1 * jax.random.normal(ks[9], (N_B_LAYERS, DIFF_HEAD_DIM), f32),
        "diff_lam_q2": 0.1 * jax.random.normal(ks[10], (N_B_LAYERS, DIFF_HEAD_DIM), f32),
        "diff_lam_k2": 0.1 * jax.random.normal(ks[11], (N_B_LAYERS, DIFF_HEAD_DIM), f32),
        "diff_subln_g": 1.0 + 0.02 * jax.random.normal(ks[12], (N_B_LAYERS, DIFF_V_DIM), f32),
        "diff_w_o": nrm(ks[13], (N_B_LAYERS, DIFF_HEADS * DIFF_V_DIM, D), DIFF_HEADS * DIFF_V_DIM, out_scale),
        "ffn_w_gate": nrm(ks[14], (N_DENSE_FFN, D, D_FF_DENSE), D),
        "ffn_w_up": nrm(ks[15], (N_DENSE_FFN, D, D_FF_DENSE), D),
        "ffn_w_down": nrm(ks[16], (N_DENSE_FFN, D_FF_DENSE, D), D_FF_DENSE, out_scale),
        "moe_w_router": nrm(ks[17], (N_MOE_FFN, D, N_EXPERTS), D),
        "moe_w_gate": nrm(ks[18], (N_MOE_FFN, N_EXPERTS, D, D_FF_EXPERT), D),
        "moe_w_up": nrm(ks[19], (N_MOE_FFN, N_EXPERTS, D, D_FF_EXPERT), D),
        "moe_w_down": nrm(ks[20], (N_MOE_FFN, N_EXPERTS, D_FF_EXPERT, D), D_FF_EXPERT, out_scale),
        "final_norm_g": 1.0 + 0.02 * jax.random.normal(ks[21], (D,), f32),
    }


def reference(x, attn_norm_g, ffn_norm_g, ret_w_in, ret_w_out, kv_norm_g, w_kv,
              diff_w_q, diff_lam_q1, diff_lam_k1, diff_lam_q2, diff_lam_k2, diff_subln_g, diff_w_o,
              ffn_w_gate, ffn_w_up, ffn_w_down,
              moe_w_router, moe_w_gate, moe_w_up, moe_w_down, final_norm_g):
    k_sh = None
    v_sh = None
    for i in range(DEPTH):
        h = rms_norm(x, attn_norm_g[i])
        if i < N_A_LAYERS:
            x = x + retention(h, ret_w_in[i], ret_w_out[i])
        else:
            j = i - N_A_LAYERS
            lambda_init = 0.8 - 0.6 * math.exp(-0.3 * i)
            x = x + diff_attention(h, k_sh, v_sh, diff_w_q[j], diff_lam_q1[j], diff_lam_k1[j],
                                   diff_lam_q2[j], diff_lam_k2[j], diff_subln_g[j], diff_w_o[j],
                                   lambda_init)
        h = rms_norm(x, ffn_norm_g[i])
        if i % 2 == 0:
            x = x + swiglu(h, ffn_w_gate[i // 2], ffn_w_up[i // 2], ffn_w_down[i // 2])
        else:
            x = x + moe_swiglu(h, moe_w_router[i // 2], moe_w_gate[i // 2],
                               moe_w_up[i // 2], moe_w_down[i // 2])
        if i == N_A_LAYERS - 1:
            k_sh, v_sh = shared_kv(x, kv_norm_g, w_kv)
    return rms_norm(x, final_norm_g)
```

```python
import functools
import math

import jax
import jax.numpy as jnp
from jax import lax
from jax.experimental import pallas as pl
from jax.experimental.pallas import tpu as pltpu

F32 = jnp.float32
BF16 = jnp.bfloat16

D_MODEL = 1024
RET_HEADS = 4
RET_QK_DIM = D_MODEL // RET_HEADS
RET_V_DIM = 2 * D_MODEL // RET_HEADS
RET_CHUNK = 128
DIFF_HEADS = 8
DIFF_HEAD_DIM = D_MODEL // DIFF_HEADS // 2
DIFF_V_DIM = 2 * DIFF_HEAD_DIM
N_EXPERTS = 8
TOP_K = 2
ROPE_THETA = 10000.0
NORM_EPS = 1e-6
SUBLN_EPS = 1e-5
LOG2E = math.log2(math.e)

LANES = 128
VMEM_LIMIT = 56 * 1024 * 1024

TOKEN_TILE = 512
ATTN_TILE = 512
EXPERT_TILE = 512
EXPERT_F_CHUNK = 512
ROUTE_LANES = LANES


def _dot(a, b):
    return jnp.dot(a, b, preferred_element_type=F32)


def _dot_nt(a, b):
    return lax.dot_general(a, b, (((1,), (1,)), ((), ())), preferred_element_type=F32)


def _dot_tn(a, b):
    return lax.dot_general(a, b, (((0,), (0,)), ((), ())), preferred_element_type=F32)


def _rms(x, eps):
    return x * lax.rsqrt(jnp.mean(x * x, axis=-1, keepdims=True) + eps)


def _silu(x):
    return x * (1.0 / (1.0 + jnp.exp(-x)))


def _const_spec(shape):
    zeros = (0,) * len(shape)
    return pl.BlockSpec(shape, lambda *_: zeros, pipeline_mode=pl.Buffered(1))


def _params(semantics):
    return pltpu.CompilerParams(dimension_semantics=semantics, vmem_limit_bytes=VMEM_LIMIT)


def _ret_proj_kernel(x_ref, g_ref, w_ref, cos_ref, sin_ref, q_ref, k_ref, v_ref, gate_ref):
    d = D_MODEL
    hb = (_rms(x_ref[...], NORM_EPS) * g_ref[...]).astype(BF16)
    c = cos_ref[...]
    s = sin_ref[...]
    half = RET_QK_DIM // 2
    for dst, col0, scale in ((q_ref, 0, 1.0), (k_ref, d, RET_QK_DIM ** -0.5)):
        y = _dot(hb, w_ref[:, col0:col0 + d])
        for h in range(RET_HEADS):
            lo = h * RET_QK_DIM
            y1 = y[:, lo:lo + half]
            y2 = y[:, lo + half:lo + 2 * half]
            dst[:, lo:lo + half] = ((y1 * c - y2 * s) * scale).astype(BF16)
            dst[:, lo + half:lo + 2 * half] = ((y2 * c + y1 * s) * scale).astype(BF16)
    for j in range(2):
        v_ref[:, j * d:(j + 1) * d] = _dot(hb, w_ref[:, (2 + j) * d:(3 + j) * d]).astype(BF16)
        gate_ref[:, j * d:(j + 1) * d] = _silu(
            _dot(hb, w_ref[:, (4 + j) * d:(5 + j) * d])).astype(BF16)


def _ret_proj(x2d, g, w_in, cos, sin, seq):
    t, d = x2d.shape
    tm = min(TOKEN_TILE, seq)
    n_seq = seq // tm
    tok = lambda i: (i, 0)
    pos = lambda i: (i % n_seq, 0)
    return pl.pallas_call(
        _ret_proj_kernel,
        grid=(t // tm,),
        in_specs=[
            pl.BlockSpec((tm, d), tok),
            _const_spec((1, d)),
            _const_spec((d, 6 * d)),
            pl.BlockSpec((tm, LANES), pos),
            pl.BlockSpec((tm, LANES), pos),
        ],
        out_specs=[
            pl.BlockSpec((tm, d), tok),
            pl.BlockSpec((tm, d), tok),
            pl.BlockSpec((tm, 2 * d), tok),
            pl.BlockSpec((tm, 2 * d), tok),
        ],
        out_shape=[
            jax.ShapeDtypeStruct((t, d), BF16),
            jax.ShapeDtypeStruct((t, d), BF16),
            jax.ShapeDtypeStruct((t, 2 * d), BF16),
            jax.ShapeDtypeStruct((t, 2 * d), BF16),
        ],
        compiler_params=_params(("parallel",)),
        name="ret_proj",
    )(x2d, g, w_in, cos, sin)


def _retention_kernel(cd_ref, q_ref, k_ref, v_ref, gate_ref, x_ref, wout_ref,
                      intra_ref, qdec_ref, kdec_ref, o_ref, state_ref, gated_ref):
    c = RET_CHUNK
    n_chunks = q_ref.shape[0] // c

    @pl.when(pl.program_id(1) == 0)
    def _():
        state_ref[...] = jnp.zeros_like(state_ref)

    for h in range(RET_HEADS):
        qk = slice(h * RET_QK_DIM, (h + 1) * RET_QK_DIM)
        vv = slice(h * RET_V_DIM, (h + 1) * RET_V_DIM)
        for n in range(n_chunks):
            rows = slice(n * c, (n + 1) * c)
            qi = q_ref[rows, qk]
            ki = k_ref[rows, qk]
            vi = v_ref[rows, vv]
            scores = _dot_nt(qi, ki) * intra_ref[h]
            intra = _dot(scores.astype(BF16), vi)
            state = state_ref[h]
            inter = _dot(qi, state.astype(BF16)) * qdec_ref[h]
            kd = (ki.astype(F32) * kdec_ref[h]).astype(BF16)
            state_ref[h] = state * cd_ref[h] + _dot_tn(kd, vi)
            o = _rms(intra + inter, NORM_EPS)
            gated_ref[rows, vv] = (gate_ref[rows, vv].astype(F32) * o).astype(BF16)
    o_ref[...] = x_ref[...] + _dot(gated_ref[...], wout_ref[...])


def _retention(chunk_decay, q, k, v, gate, x, w_out, intra_decay, q_decay, k_decay):
    b, s, d = x.shape
    ts = min(TOKEN_TILE, s)
    blk = lambda w: pl.BlockSpec((None, ts, w), lambda bi, si: (bi, si, 0))
    return pl.pallas_call(
        _retention_kernel,
        grid=(b, s // ts),
        in_specs=[
            pl.BlockSpec(memory_space=pltpu.SMEM),
            blk(d), blk(d), blk(2 * d), blk(2 * d), blk(d),
            _const_spec((2 * d, d)),
            _const_spec((RET_HEADS, RET_CHUNK, RET_CHUNK)),
            _const_spec((RET_HEADS, RET_CHUNK, 1)),
            _const_spec((RET_HEADS, RET_CHUNK, 1)),
        ],
        out_specs=blk(d),
        out_shape=jax.ShapeDtypeStruct((b, s, d), F32),
        scratch_shapes=[
            pltpu.VMEM((RET_HEADS, RET_QK_DIM, RET_V_DIM), F32),
            pltpu.VMEM((ts, 2 * d), BF16),
        ],
        compiler_params=_params(("parallel", "arbitrary")),
        name="retention",
    )(chunk_decay, q, k, v, gate, x, w_out, intra_decay, q_decay, k_decay)


def _ffn_kernel(x_ref, g_ref, wg_ref, wu_ref, wd_ref, o_ref, *, f_chunks):
    x = x_ref[...]
    hb = (_rms(x, NORM_EPS) * g_ref[...]).astype(BF16)
    acc = x
    for lo, hi in f_chunks:
        a = _dot(hb, wg_ref[:, lo:hi])
        u = _dot(hb, wu_ref[:, lo:hi])
        acc = acc + _dot((_silu(a) * u).astype(BF16), wd_ref[lo:hi, :])
    o_ref[...] = acc


def _chunks(total, size):
    return tuple((lo, min(lo + size, total)) for lo in range(0, total, size))


def _ffn(x2d, g, wg, wu, wd):
    t, d = x2d.shape
    f = wg.shape[1]
    tm = min(TOKEN_TILE, t)
    tok = lambda i: (i, 0)
    return pl.pallas_call(
        functools.partial(_ffn_kernel, f_chunks=_chunks(f, 1024)),
        grid=(t // tm,),
        in_specs=[
            pl.BlockSpec((tm, d), tok),
            _const_spec((1, d)),
            _const_spec((d, f)),
            _const_spec((d, f)),
            _const_spec((f, d)),
        ],
        out_specs=pl.BlockSpec((tm, d), tok),
        out_shape=jax.ShapeDtypeStruct((t, d), F32),
        compiler_params=_params(("parallel",)),
        name="dense_ffn",
    )(x2d, g, wg, wu, wd)


def _rope64(y, cos, sin_a, sin_b):
    half = DIFF_HEAD_DIM // 2
    return (y * cos + pltpu.roll(y, LANES - half, 1) * sin_a
            + pltpu.roll(y, half, 1) * sin_b)


def _kvq_proj_kernel(x_ref, gkv_ref, gq_ref, wkv_ref, wq_ref, cos_ref, sa_ref, sb_ref,
                     k_ref, v_ref, q_ref):
    d = D_MODEL
    xn = _rms(x_ref[...], NORM_EPS)
    hkv = (xn * gkv_ref[...]).astype(BF16)
    hq = (xn * gq_ref[...]).astype(BF16)
    cos = cos_ref[...]
    sa = sa_ref[...]
    sb = sb_ref[...]
    yk = _dot(hkv, wkv_ref[:, 0:d])
    yq = _dot(hq, wq_ref[...])
    q_scale = DIFF_HEAD_DIM ** -0.5 * LOG2E
    for j in range(d // LANES):
        cols = slice(j * LANES, (j + 1) * LANES)
        k_ref[:, cols] = _rope64(yk[:, cols], cos, sa, sb).astype(BF16)
        q_ref[:, cols] = (_rope64(yq[:, cols], cos, sa, sb) * q_scale).astype(BF16)
    v_ref[...] = _dot(hkv, wkv_ref[:, d:2 * d]).astype(BF16)


def _kvq_proj(x2d, gkv, gq, wkv, wq, cos, sin_a, sin_b, seq):
    t, d = x2d.shape
    tm = min(TOKEN_TILE, seq)
    n_seq = seq // tm
    tok = lambda i: (i, 0)
    pos = lambda i: (i % n_seq, 0)
    return pl.pallas_call(
        _kvq_proj_kernel,
        grid=(t // tm,),
        in_specs=[
            pl.BlockSpec((tm, d), tok),
            _const_spec((1, d)),
            _const_spec((1, d)),
            _const_spec((d, 2 * d)),
            _const_spec((d, d)),
            pl.BlockSpec((tm, LANES), pos),
            pl.BlockSpec((tm, LANES), pos),
            pl.BlockSpec((tm, LANES), pos),
        ],
        out_specs=[pl.BlockSpec((tm, d), tok)] * 3,
        out_shape=[jax.ShapeDtypeStruct((t, d), BF16)] * 3,
        compiler_params=_params(("parallel",)),
        name="kvq_proj",
    )(x2d, gkv, gq, wkv, wq, cos, sin_a, sin_b)


def _attn_kernel(lam_ref, q_ref, k_ref, v_ref, g_ref, o_ref, qq_ref, m_ref, l_ref, acc_ref,
                 *, out_scale):
    tq = q_ref.shape[0]
    tk = tq
    qi = pl.program_id(2)

    q = q_ref[...]
    lane = lax.broadcasted_iota(jnp.int32, q.shape, 1)
    zero = jnp.zeros_like(q)
    qq_ref[0:tq, :] = jnp.where(lane < DIFF_HEAD_DIM, q, zero)
    qq_ref[tq:2 * tq, :] = jnp.where(lane >= DIFF_HEAD_DIM, q, zero)
    m_ref[...] = jnp.full_like(m_ref, -jnp.inf)
    l_ref[...] = jnp.zeros_like(l_ref)
    acc_ref[...] = jnp.zeros_like(acc_ref)

    def block(j, masked):
        start = pl.multiple_of(j * tk, tk)
        kb = k_ref[pl.ds(start, tk), :]
        vb = v_ref[pl.ds(start, tk), :]
        s = _dot_nt(qq_ref[...], kb)
        if masked:
            row = lax.broadcasted_iota(jnp.int32, (tq, tk), 0)
            col = lax.broadcasted_iota(jnp.int32, (tq, tk), 1)
            keep = col <= row
            keep = jnp.concatenate([keep, keep], axis=0)
            s = jnp.where(keep, s, -1e30)
        m_prev = m_ref[...]
        m_new = jnp.maximum(m_prev, jnp.max(s, axis=1, keepdims=True))
        alpha = jnp.exp2(m_prev - m_new)
        p = jnp.exp2(s - m_new)
        l_ref[...] = alpha * l_ref[...] + jnp.sum(p, axis=1, keepdims=True)
        acc_ref[...] = alpha * acc_ref[...] + _dot(p.astype(BF16), vb)
        m_ref[...] = m_new

    def body(j, carry):
        block(j, False)
        return carry

    lax.fori_loop(0, qi, body, 0)
    block(qi, True)

    o = acc_ref[...] / l_ref[...]
    out = o[0:tq, :] - lam_ref[0] * o[tq:2 * tq, :]
    out = _rms(out, SUBLN_EPS) * g_ref[...] * out_scale
    o_ref[...] = out.astype(BF16)


def _diff_attention(lam, q, k, v, subln_g, lambda_init):
    b, s, d = q.shape
    tq = min(ATTN_TILE, s)
    return pl.pallas_call(
        functools.partial(_attn_kernel, out_scale=1.0 - lambda_init),
        grid=(b, DIFF_HEADS, s // tq),
        in_specs=[
            pl.BlockSpec(memory_space=pltpu.SMEM),
            pl.BlockSpec((None, tq, LANES), lambda bi, h, i: (bi, i, h)),
            pl.BlockSpec((None, s, LANES), lambda bi, h, i: (bi, 0, h)),
            pl.BlockSpec((None, s, LANES), lambda bi, h, i: (bi, 0, h)),
            _const_spec((1, LANES)),
        ],
        out_specs=pl.BlockSpec((None, tq, LANES), lambda bi, h, i: (bi, i, h)),
        out_shape=jax.ShapeDtypeStruct((b, s, d), BF16),
        scratch_shapes=[
            pltpu.VMEM((2 * tq, LANES), BF16),
            pltpu.VMEM((2 * tq, 1), F32),
            pltpu.VMEM((2 * tq, 1), F32),
            pltpu.VMEM((2 * tq, LANES), F32),
        ],
        compiler_params=_params(("parallel", "parallel", "arbitrary")),
        name="diff_attention",
    )(lam, q, k, v, subln_g)


def _split_bf16(x):
    hi = x.astype(BF16)
    lo = (x - hi.astype(F32)).astype(BF16)
    return hi, lo


def _oproj_router_kernel(a_ref, x_ref, wo_ref, g_ref, wr_hi_ref, wr_lo_ref,
                         x_out_ref, h_ref, route_ref):
    x = x_ref[...] + _dot(a_ref[...], wo_ref[...])
    x_out_ref[...] = x
    h = _rms(x, NORM_EPS) * g_ref[...]
    h_hi, h_lo = _split_bf16(h)
    h_ref[...] = h_hi
    w_hi = wr_hi_ref[...]
    logits = _dot(h_hi, w_hi) + (_dot(h_lo, w_hi) + _dot(h_hi, wr_lo_ref[...]))
    lane = lax.broadcasted_iota(jnp.int32, logits.shape, 1)
    lane_f = lane.astype(F32)
    neg = -jnp.inf
    lg = jnp.where(lane < N_EXPERTS, logits, neg)
    m1 = jnp.max(lg, axis=1, keepdims=True)
    i1 = jnp.min(jnp.where(lg == m1, lane_f, float(ROUTE_LANES)), axis=1, keepdims=True)
    lg2 = jnp.where(lane_f == i1, neg, lg)
    m2 = jnp.max(lg2, axis=1, keepdims=True)
    i2 = jnp.min(jnp.where(lg2 == m2, lane_f, float(ROUTE_LANES)), axis=1, keepdims=True)
    e = jnp.exp(m2 - m1)
    w1 = 1.0 / (1.0 + e)
    w2 = e / (1.0 + e)
    route = jnp.where(lane == 0, i1, jnp.where(lane == 1, i2,
                      jnp.where(lane == 2, w1, jnp.where(lane == 3, w2, 0.0))))
    route_ref[...] = route


def _oproj_router(attn2d, x2d, wo, g, wr_hi, wr_lo):
    t, d = x2d.shape
    tm = min(TOKEN_TILE, t)
    tok = lambda i: (i, 0)
    return pl.pallas_call(
        _oproj_router_kernel,
        grid=(t // tm,),
        in_specs=[
            pl.BlockSpec((tm, d), tok),
            pl.BlockSpec((tm, d), tok),
            _const_spec((d, d)),
            _const_spec((1, d)),
            _const_spec((d, ROUTE_LANES)),
            _const_spec((d, ROUTE_LANES)),
        ],
        out_specs=[
            pl.BlockSpec((tm, d), tok),
            pl.BlockSpec((tm, d), tok),
            pl.BlockSpec((tm, ROUTE_LANES), tok),
        ],
        out_shape=[
            jax.ShapeDtypeStruct((t, d), F32),
            jax.ShapeDtypeStruct((t, d), BF16),
            jax.ShapeDtypeStruct((t, ROUTE_LANES), F32),
        ],
        compiler_params=_params(("parallel",)),
        name="oproj_router",
    )(attn2d, x2d, wo, g, wr_hi, wr_lo)


def _expert_kernel(te_ref, nt_ref, xs_ref, rw_ref, wg_ref, wu_ref, wd_ref, y_ref, *, f_chunks):
    i = pl.program_id(0)

    @pl.when(i < nt_ref[0])
    def _():
        xs = xs_ref[...]
        acc = jnp.zeros(y_ref.shape, F32)
        for lo, hi in f_chunks:
            a = _dot(xs, wg_ref[:, lo:hi])
            u = _dot(xs, wu_ref[:, lo:hi])
            acc = acc + _dot((_silu(a) * u).astype(BF16), wd_ref[lo:hi, :])
        y_ref[...] = (acc * rw_ref[...]).astype(y_ref.dtype)

    @pl.when(i >= nt_ref[0])
    def _():
        y_ref[...] = jnp.zeros_like(y_ref)


def _experts(tile_expert, n_tiles_used, xs, row_w, wg, wu, wd):
    p, d = xs.shape
    f = wg.shape[2]
    tm = EXPERT_TILE
    n_tiles = p // tm

    def row(i, te, nt):
        return (jnp.minimum(i, nt[0] - 1), 0)

    def wsel(i, te, nt):
        return (te[jnp.minimum(i, nt[0] - 1)], 0, 0)

    grid_spec = pltpu.PrefetchScalarGridSpec(
        num_scalar_prefetch=2,
        grid=(n_tiles,),
        in_specs=[
            pl.BlockSpec((tm, d), row),
            pl.BlockSpec((tm, 1), row),
            pl.BlockSpec((None, d, f), wsel, pipeline_mode=pl.Buffered(1)),
            pl.BlockSpec((None, d, f), wsel, pipeline_mode=pl.Buffered(1)),
            pl.BlockSpec((None, f, d), wsel, pipeline_mode=pl.Buffered(1)),
        ],
        out_specs=pl.BlockSpec((tm, d), lambda i, te, nt: (i, 0)),
    )
    return pl.pallas_call(
        functools.partial(_expert_kernel, f_chunks=_chunks(f, EXPERT_F_CHUNK)),
        grid_spec=grid_spec,
        out_shape=jax.ShapeDtypeStruct((p, d), BF16),
        compiler_params=_params(("arbitrary",)),
        name="experts",
    )(tile_expert, n_tiles_used, xs, row_w, wg, wu, wd)


def _route_tables(route, tm):
    t = route.shape[0]
    n_pairs = t * TOP_K
    n_tiles = n_pairs // tm + N_EXPERTS - 1
    p = n_tiles * tm
    e_pair = route[:, 0:TOP_K].astype(jnp.int32).reshape(n_pairs)
    w_pair = route[:, TOP_K:2 * TOP_K].reshape(n_pairs)
    onehot = (e_pair[:, None] == jnp.arange(N_EXPERTS, dtype=jnp.int32)[None, :]).astype(jnp.int32)
    counts = jnp.sum(onehot, axis=0)
    rank = jnp.sum((jnp.cumsum(onehot, axis=0) - onehot) * onehot, axis=1)
    tiles_per = (counts + tm - 1) // tm
    tile_end = jnp.cumsum(tiles_per)
    tile_start = tile_end - tiles_per
    dest = tile_start[e_pair] * tm + rank
    row_token = jnp.zeros((p,), jnp.int32).at[dest].set(
        jnp.arange(n_pairs, dtype=jnp.int32) // TOP_K, unique_indices=True)
    row_w = jnp.zeros((p,), F32).at[dest].set(w_pair, unique_indices=True)
    tile_expert = jnp.minimum(
        jnp.searchsorted(tile_end, jnp.arange(n_tiles, dtype=jnp.int32), side="right"),
        N_EXPERTS - 1).astype(jnp.int32)
    n_used = tile_end[-1:].astype(jnp.int32)
    return row_token, row_w.reshape(p, 1), tile_expert, n_used, dest.reshape(t, TOP_K)


def _final_kernel(x_ref, ya_ref, yb_ref, g_ref, o_ref):
    x = x_ref[...] + (ya_ref[...].astype(F32) + yb_ref[...].astype(F32))
    o_ref[...] = _rms(x, NORM_EPS) * g_ref[...]


def _final(x2d, ya, yb, g):
    t, d = x2d.shape
    tm = min(2 * TOKEN_TILE, t)
    tok = lambda i: (i, 0)
    return pl.pallas_call(
        _final_kernel,
        grid=(t // tm,),
        in_specs=[pl.BlockSpec((tm, d), tok)] * 3 + [_const_spec((1, d))],
        out_specs=pl.BlockSpec((tm, d), tok),
        out_shape=jax.ShapeDtypeStruct((t, d), F32),
        compiler_params=_params(("parallel",)),
        name="final_norm",
    )(x2d, ya, yb, g)


def _rope_angles(seq, dim):
    inv = 1.0 / (ROPE_THETA ** (jnp.arange(0, dim, 2, dtype=F32) / dim))
    return jnp.arange(seq, dtype=F32)[:, None] * inv[None, :]


def _retention_decays():
    h, c = RET_HEADS, RET_CHUNK
    log_gamma = jnp.log1p(-jnp.exp2(-5.0 - jnp.arange(h, dtype=F32)))
    pos = jnp.arange(c, dtype=F32)
    rel = pos[:, None] - pos[None, :]
    intra = jnp.where(rel >= 0, jnp.exp(log_gamma[:, None, None] * jnp.maximum(rel, 0.0)), 0.0)
    q_decay = jnp.exp(log_gamma[:, None] * (pos + 1.0))[:, :, None]
    k_decay = jnp.exp(log_gamma[:, None] * (c - 1.0 - pos))[:, :, None]
    chunk_decay = jnp.exp(log_gamma * c)
    return chunk_decay, intra, q_decay, k_decay


def kernel(x, attn_norm_g, ffn_norm_g, ret_w_in, ret_w_out, kv_norm_g, w_kv, diff_w_q,
           diff_lam_q1, diff_lam_k1, diff_lam_q2, diff_lam_k2, diff_subln_g, diff_w_o,
           ffn_w_gate, ffn_w_up, ffn_w_down, moe_w_router, moe_w_gate, moe_w_up, moe_w_down,
           final_norm_g):
    b, s, d = x.shape
    t = b * s
    row = lambda g: g.reshape(1, -1).astype(F32)
    bf = lambda w: w.astype(BF16)

    ang = _rope_angles(s, RET_QK_DIM)
    q, k, v, gate = _ret_proj(x.reshape(t, d), row(attn_norm_g[0]), bf(ret_w_in[0]),
                              jnp.cos(ang), jnp.sin(ang), s)
    chunk_decay, intra_decay, q_decay, k_decay = _retention_decays()
    x1 = _retention(chunk_decay, q.reshape(b, s, d), k.reshape(b, s, d),
                    v.reshape(b, s, 2 * d), gate.reshape(b, s, 2 * d), x,
                    bf(ret_w_out[0]), intra_decay, q_decay, k_decay)
    x2 = _ffn(x1.reshape(t, d), row(ffn_norm_g[0]), bf(ffn_w_gate[0]), bf(ffn_w_up[0]),
              bf(ffn_w_down[0]))

    ang = _rope_angles(s, DIFF_HEAD_DIM)
    ang = jnp.concatenate([ang] * (LANES // ang.shape[1]), axis=-1)
    first_half = (jnp.arange(LANES) % DIFF_HEAD_DIM) < DIFF_HEAD_DIM // 2
    sin = jnp.sin(ang)
    sin_a = jnp.where(first_half[None, :], -sin, 0.0)
    sin_b = jnp.where(first_half[None, :], 0.0, sin)
    ks, vs, qs = _kvq_proj(x2, row(kv_norm_g), row(attn_norm_g[1]), bf(w_kv), bf(diff_w_q[0]),
                           jnp.cos(ang), sin_a, sin_b, s)
    lambda_init = 0.8 - 0.6 * math.exp(-0.3 * 1)
    lam = (jnp.exp(jnp.sum(diff_lam_q1[0].astype(F32) * diff_lam_k1[0].astype(F32)))
           - jnp.exp(jnp.sum(diff_lam_q2[0].astype(F32) * diff_lam_k2[0].astype(F32)))
           + lambda_init).reshape(1)
    attn = _diff_attention(lam, qs.reshape(b, s, d), ks.reshape(b, s, d), vs.reshape(b, s, d),
                           row(diff_subln_g[0]), lambda_init)

    wr = jnp.zeros((d, ROUTE_LANES), F32).at[:, :N_EXPERTS].set(moe_w_router[0])
    wr_hi = wr.astype(BF16)
    wr_lo = (wr - wr_hi.astype(F32)).astype(BF16)
    x3, h, route = _oproj_router(attn.reshape(t, d), x2, bf(diff_w_o[0]), row(ffn_norm_g[1]),
                                 wr_hi, wr_lo)
    row_token, row_w, tile_expert, n_used, dest = _route_tables(route, EXPERT_TILE)
    xs = jnp.take(h, row_token, axis=0)
    y = _experts(tile_expert, n_used, xs, row_w, bf(moe_w_gate[0]), bf(moe_w_up[0]),
                 bf(moe_w_down[0]))
    ya = jnp.take(y, dest[:, 0], axis=0)
    yb = jnp.take(y, dest[:, 1], axis=0)
    out = _final(x3, ya, yb, row(final_norm_g))
    return out.reshape(b, s, d)
```

```python
import functools
import math

import jax
import jax.numpy as jnp
from jax import lax
from jax.experimental import pallas as pl
from jax.experimental.pallas import tpu as pltpu

F32 = jnp.float32
BF16 = jnp.bfloat16

D_MODEL = 1024
RET_HEADS = 4
RET_QK_DIM = D_MODEL // RET_HEADS
RET_V_DIM = 2 * D_MODEL // RET_HEADS
RET_CHUNK = 128
DIFF_HEADS = 8
DIFF_HEAD_DIM = D_MODEL // DIFF_HEADS // 2
DIFF_V_DIM = 2 * DIFF_HEAD_DIM
N_EXPERTS = 8
TOP_K = 2
ROPE_THETA = 10000.0
NORM_EPS = 1e-6
SUBLN_EPS = 1e-5
LOG2E = math.log2(math.e)

LANES = 128
VMEM_LIMIT = 56 * 1024 * 1024

TOKEN_TILE = 512
ATTN_TILE = 512
EXPERT_TILE = 512
EXPERT_F_CHUNK = 512
ROUTE_LANES = LANES


def _dot(a, b):
    return jnp.dot(a, b, preferred_element_type=F32)


def _dot_nt(a, b):
    return lax.dot_general(a, b, (((1,), (1,)), ((), ())), preferred_element_type=F32)


def _dot_tn(a, b):
    return lax.dot_general(a, b, (((0,), (0,)), ((), ())), preferred_element_type=F32)


def _rms(x, eps):
    return x * lax.rsqrt(jnp.mean(x * x, axis=-1, keepdims=True) + eps)


def _silu(x):
    return x * (1.0 / (1.0 + jnp.exp(-x)))


def _const_spec(shape):
    zeros = (0,) * len(shape)
    return pl.BlockSpec(shape, lambda *_: zeros, pipeline_mode=pl.Buffered(1))


def _params(semantics):
    return pltpu.CompilerParams(dimension_semantics=semantics, vmem_limit_bytes=VMEM_LIMIT)


def _ret_proj_kernel(x_ref, g_ref, w_ref, cos_ref, sin_ref, q_ref, k_ref, v_ref, gate_ref):
    d = D_MODEL
    hb = (_rms(x_ref[...], NORM_EPS) * g_ref[...]).astype(BF16)
    c = cos_ref[...]
    s = sin_ref[...]
    half = RET_QK_DIM // 2
    for dst, col0, scale in ((q_ref, 0, 1.0), (k_ref, d, RET_QK_DIM ** -0.5)):
        y = _dot(hb, w_ref[:, col0:col0 + d])
        for h in range(RET_HEADS):
            lo = h * RET_QK_DIM
            y1 = y[:, lo:lo + half]
            y2 = y[:, lo + half:lo + 2 * half]
            dst[:, lo:lo + half] = ((y1 * c - y2 * s) * scale).astype(BF16)
            dst[:, lo + half:lo + 2 * half] = ((y2 * c + y1 * s) * scale).astype(BF16)
    for j in range(2):
        v_ref[:, j * d:(j + 1) * d] = _dot(hb, w_ref[:, (2 + j) * d:(3 + j) * d]).astype(BF16)
        gate_ref[:, j * d:(j + 1) * d] = _silu(
            _dot(hb, w_ref[:, (4 + j) * d:(5 + j) * d])).astype(BF16)


def _ret_proj(x2d, g, w_in, cos, sin, seq):
    t, d = x2d.shape
    tm = min(TOKEN_TILE, seq)
    n_seq = seq // tm
    tok = lambda i: (i, 0)
    pos = lambda i: (i % n_seq, 0)
    return pl.pallas_call(
        _ret_proj_kernel,
        grid=(t // tm,),
        in_specs=[
            pl.BlockSpec((tm, d), tok),
            _const_spec((1, d)),
            _const_spec((d, 6 * d)),
            pl.BlockSpec((tm, LANES), pos),
            pl.BlockSpec((tm, LANES), pos),
        ],
        out_specs=[
            pl.BlockSpec((tm, d), tok),
            pl.BlockSpec((tm, d), tok),
            pl.BlockSpec((tm, 2 * d), tok),
            pl.BlockSpec((tm, 2 * d), tok),
        ],
        out_shape=[
            jax.ShapeDtypeStruct((t, d), BF16),
            jax.ShapeDtypeStruct((t, d), BF16),
            jax.ShapeDtypeStruct((t, 2 * d), BF16),
            jax.ShapeDtypeStruct((t, 2 * d), BF16),
        ],
        compiler_params=_params(("parallel",)),
        name="ret_proj",
    )(x2d, g, w_in, cos, sin)


def _retention_kernel(cd_ref, q_ref, k_ref, v_ref, gate_ref, x_ref, wout_ref,
                      intra_ref, qdec_ref, kdec_ref, o_ref, state_ref, gated_ref):
    c = RET_CHUNK
    n_chunks = q_ref.shape[0] // c

    @pl.when(pl.program_id(1) == 0)
    def _():
        state_ref[...] = jnp.zeros_like(state_ref)

    for h in range(RET_HEADS):
        qk = slice(h * RET_QK_DIM, (h + 1) * RET_QK_DIM)
        vv = slice(h * RET_V_DIM, (h + 1) * RET_V_DIM)
        for n in range(n_chunks):
            rows = slice(n * c, (n + 1) * c)
            qi = q_ref[rows, qk]
            ki = k_ref[rows, qk]
            vi = v_ref[rows, vv]
            scores = _dot_nt(qi, ki) * intra_ref[h]
            intra = _dot(scores.astype(BF16), vi)
            state = state_ref[h]
            inter = _dot(qi, state.astype(BF16)) * qdec_ref[h]
            kd = (ki.astype(F32) * kdec_ref[h]).astype(BF16)
            state_ref[h] = state * cd_ref[h] + _dot_tn(kd, vi)
            o = _rms(intra + inter, NORM_EPS)
            gated_ref[rows, vv] = (gate_ref[rows, vv].astype(F32) * o).astype(BF16)
    o_ref[...] = x_ref[...] + _dot(gated_ref[...], wout_ref[...])


def _retention(chunk_decay, q, k, v, gate, x, w_out, intra_decay, q_decay, k_decay):
    b, s, d = x.shape
    ts = min(TOKEN_TILE, s)
    blk = lambda w: pl.BlockSpec((None, ts, w), lambda bi, si: (bi, si, 0))
    return pl.pallas_call(
        _retention_kernel,
        grid=(b, s // ts),
        in_specs=[
            pl.BlockSpec(memory_space=pltpu.SMEM),
            blk(d), blk(d), blk(2 * d), blk(2 * d), blk(d),
            _const_spec((2 * d, d)),
            _const_spec((RET_HEADS, RET_CHUNK, RET_CHUNK)),
            _const_spec((RET_HEADS, RET_CHUNK, 1)),
            _const_spec((RET_HEADS, RET_CHUNK, 1)),
        ],
        out_specs=blk(d),
        out_shape=jax.ShapeDtypeStruct((b, s, d), F32),
        scratch_shapes=[
            pltpu.VMEM((RET_HEADS, RET_QK_DIM, RET_V_DIM), F32),
            pltpu.VMEM((ts, 2 * d), BF16),
        ],
        compiler_params=_params(("parallel", "arbitrary")),
        name="retention",
    )(chunk_decay, q, k, v, gate, x, w_out, intra_decay, q_decay, k_decay)


def _ffn_kernel(x_ref, g_ref, wg_ref, wu_ref, wd_ref, o_ref, *, f_chunks):
    x = x_ref[...]
    hb = (_rms(x, NORM_EPS) * g_ref[...]).astype(BF16)
    acc = x
    for lo, hi in f_chunks:
        a = _dot(hb, wg_ref[:, lo:hi])
        u = _dot(hb, wu_ref[:, lo:hi])
        acc = acc + _dot((_silu(a) * u).astype(BF16), wd_ref[lo:hi, :])
    o_ref[...] = acc


def _chunks(total, size):
    return tuple((lo, min(lo + size, total)) for lo in range(0, total, size))


def _ffn(x2d, g, wg, wu, wd):
    t, d = x2d.shape
    f = wg.shape[1]
    tm = min(TOKEN_TILE, t)
    tok = lambda i: (i, 0)
    return pl.pallas_call(
        functools.partial(_ffn_kernel, f_chunks=_chunks(f, 1024)),
        grid=(t // tm,),
        in_specs=[
            pl.BlockSpec((tm, d), tok),
            _const_spec((1, d)),
            _const_spec((d, f)),
            _const_spec((d, f)),
            _const_spec((f, d)),
        ],
        out_specs=pl.BlockSpec((tm, d), tok),
        out_shape=jax.ShapeDtypeStruct((t, d), F32),
        compiler_params=_params(("parallel",)),
        name="dense_ffn",
    )(x2d, g, wg, wu, wd)


def _rope64(y, cos, sin_a, sin_b):
    half = DIFF_HEAD_DIM // 2
    return (y * cos + pltpu.roll(y, LANES - half, 1) * sin_a
            + pltpu.roll(y, half, 1) * sin_b)


def _kvq_proj_kernel(x_ref, gkv_ref, gq_ref, wk_ref, wvt_ref, wqt_ref, cos_ref, sa_ref, sb_ref,
                     cost_ref, sint_ref, k_ref, vt_ref, qt_ref):
    d = D_MODEL
    xn = _rms(x_ref[...], NORM_EPS)
    hkv = (xn * gkv_ref[...]).astype(BF16)
    hq = (xn * gq_ref[...]).astype(BF16)
    cos = cos_ref[...]
    sa = sa_ref[...]
    sb = sb_ref[...]
    yk = _dot(hkv, wk_ref[...])
    for j in range(d // LANES):
        cols = slice(j * LANES, (j + 1) * LANES)
        k_ref[:, cols] = _rope64(yk[:, cols], cos, sa, sb).astype(BF16)

    yvt = _dot_nt(wvt_ref[...], hkv)
    for h in range(DIFF_HEADS):
        vt_ref[h] = yvt[h * DIFF_V_DIM:(h + 1) * DIFF_V_DIM, :].astype(BF16)

    yqt = _dot_nt(wqt_ref[...], hq)
    ct = cost_ref[...]
    st = sint_ref[...]
    half = DIFF_HEAD_DIM // 2
    q_scale = DIFF_HEAD_DIM ** -0.5 * LOG2E
    for g in range(d // DIFF_HEAD_DIM):
        r0 = g * DIFF_HEAD_DIM
        a = yqt[r0:r0 + half, :]
        b = yqt[r0 + half:r0 + 2 * half, :]
        qt_ref[r0:r0 + half, :] = ((a * ct - b * st) * q_scale).astype(BF16)
        qt_ref[r0 + half:r0 + 2 * half, :] = ((b * ct + a * st) * q_scale).astype(BF16)


def _kvq_proj(x, gkv, gq, wk, wvt, wqt, cos, sin_a, sin_b, cos_t, sin_t):
    b, s, d = x.shape
    tm = min(ATTN_TILE, s)
    n_seq = s // tm
    half = DIFF_HEAD_DIM // 2
    pos = lambda bi, i: (i, 0)
    pos_t = lambda bi, i: (0, i)
    return pl.pallas_call(
        _kvq_proj_kernel,
        grid=(b, n_seq),
        in_specs=[
            pl.BlockSpec((None, tm, d), lambda bi, i: (bi, i, 0)),
            _const_spec((1, d)),
            _const_spec((1, d)),
            _const_spec((d, d)),
            _const_spec((d, d)),
            _const_spec((d, d)),
            pl.BlockSpec((tm, LANES), pos),
            pl.BlockSpec((tm, LANES), pos),
            pl.BlockSpec((tm, LANES), pos),
            pl.BlockSpec((half, tm), pos_t),
            pl.BlockSpec((half, tm), pos_t),
        ],
        out_specs=[
            pl.BlockSpec((None, tm, d), lambda bi, i: (bi, i, 0)),
            pl.BlockSpec((None, DIFF_HEADS, None, DIFF_V_DIM, tm), lambda bi, i: (bi, 0, i, 0, 0)),
            pl.BlockSpec((None, d, tm), lambda bi, i: (bi, 0, i)),
        ],
        out_shape=[
            jax.ShapeDtypeStruct((b, s, d), BF16),
            jax.ShapeDtypeStruct((b, DIFF_HEADS, n_seq, DIFF_V_DIM, tm), BF16),
            jax.ShapeDtypeStruct((b, d, s), BF16),
        ],
        compiler_params=_params(("parallel", "parallel")),
        name="kvq_proj",
    )(x, gkv, gq, wk, wvt, wqt, cos, sin_a, sin_b, cos_t, sin_t)


MXU_COLS = 256
MASKED_LOGIT = -1e30


def _attn_kernel(lam_ref, qt_ref, k_ref, vt_ref, g_ref, o_ref, qq_ref, m_ref, l_ref, acc_ref,
                 s0_ref, *, out_scale):
    tq = qt_ref.shape[1]
    tk = vt_ref.shape[2]
    assert tq == tk
    r = 2 * tq
    qi = pl.program_id(2)

    qt = qt_ref[...]
    feat = lax.broadcasted_iota(jnp.int32, qt.shape, 0)
    zero = jnp.zeros_like(qt)
    qq_ref[:, 0:tq] = jnp.where(feat < DIFF_HEAD_DIM, qt, zero)
    qq_ref[:, tq:r] = jnp.where(feat >= DIFF_HEAD_DIM, qt, zero)
    m_ref[...] = jnp.full_like(m_ref, MASKED_LOGIT)
    l_ref[...] = jnp.zeros_like(l_ref)
    acc_ref[...] = jnp.zeros_like(acc_ref)

    n_groups = r // MXU_COLS

    def logits(j, c):
        start = pl.multiple_of(j * tk, tk)
        return _dot(k_ref[pl.ds(start, tk), :],
                    qq_ref[:, c * MXU_COLS:(c + 1) * MXU_COLS])

    def block(j, is_diagonal):
        vt = vt_ref[j]
        s_next = s0_ref[...]
        for c in range(n_groups):
            cols = slice(c * MXU_COLS, (c + 1) * MXU_COLS)
            s = s_next
            if c + 1 < n_groups:
                s_next = logits(j, c + 1)
            elif not is_diagonal:
                s0_ref[...] = logits(j + 1, 0)
            if is_diagonal:
                key = lax.broadcasted_iota(jnp.int32, s.shape, 0)
                qry = lax.broadcasted_iota(jnp.int32, s.shape, 1) + (c * MXU_COLS) % tq
                s = jnp.where(key <= qry, s, MASKED_LOGIT)
            m_prev = m_ref[:, cols]
            m_new = jnp.maximum(m_prev, jnp.max(s, axis=0, keepdims=True))
            alpha = jnp.exp2(m_prev - m_new)
            p = jnp.exp2(s - m_new)
            l_ref[:, cols] = alpha * l_ref[:, cols] + jnp.sum(p, axis=0, keepdims=True)
            acc_ref[:, cols] = alpha * acc_ref[:, cols] + _dot(vt, p.astype(BF16))
            m_ref[:, cols] = m_new

    def body(j, carry):
        block(j, False)
        return carry

    s0_ref[...] = logits(0, 0)
    lax.fori_loop(0, qi, body, 0)
    block(qi, True)

    o = acc_ref[...] * (1.0 / l_ref[...])
    out = o[:, 0:tq] - lam_ref[0] * o[:, tq:r]
    ms = jnp.mean(out * out, axis=0, keepdims=True)
    out = out * lax.rsqrt(ms + SUBLN_EPS) * g_ref[...] * out_scale
    o_ref[...] = out.T.astype(BF16)


def _diff_attention(lam, qt, k, vt, subln_g_cols, lambda_init):
    b, s, d = k.shape
    tq = vt.shape[-1]
    n_blk = s // tq
    return pl.pallas_call(
        functools.partial(_attn_kernel, out_scale=1.0 - lambda_init),
        grid=(b, DIFF_HEADS, n_blk),
        in_specs=[
            pl.BlockSpec(memory_space=pltpu.SMEM),
            pl.BlockSpec((None, LANES, tq), lambda bi, h, i: (bi, h, i)),
            pl.BlockSpec((None, s, LANES), lambda bi, h, i: (bi, 0, h)),
            pl.BlockSpec((None, None, n_blk, DIFF_V_DIM, tq), lambda bi, h, i: (bi, h, 0, 0, 0)),
            _const_spec((DIFF_V_DIM, tq)),
        ],
        out_specs=pl.BlockSpec((None, tq, LANES), lambda bi, h, i: (bi, i, h)),
        out_shape=jax.ShapeDtypeStruct((b, s, d), BF16),
        scratch_shapes=[
            pltpu.VMEM((LANES, 2 * tq), BF16),
            pltpu.VMEM((1, 2 * tq), F32),
            pltpu.VMEM((1, 2 * tq), F32),
            pltpu.VMEM((DIFF_V_DIM, 2 * tq), F32),
            pltpu.VMEM((tq, MXU_COLS), F32),
        ],
        compiler_params=_params(("parallel", "parallel", "arbitrary")),
        name="diff_attention",
    )(lam, qt, k, vt, subln_g_cols)


def _split_bf16(x):
    hi = x.astype(BF16)
    lo = (x - hi.astype(F32)).astype(BF16)
    return hi, lo


def _oproj_router_kernel(a_ref, x_ref, wo_ref, g_ref, wr_hi_ref, wr_lo_ref,
                         x_out_ref, h_ref, route_ref):
    x = x_ref[...] + _dot(a_ref[...], wo_ref[...])
    x_out_ref[...] = x
    h = _rms(x, NORM_EPS) * g_ref[...]
    h_hi, h_lo = _split_bf16(h)
    h_ref[...] = h_hi
    w_hi = wr_hi_ref[...]
    logits = _dot(h_hi, w_hi) + (_dot(h_lo, w_hi) + _dot(h_hi, wr_lo_ref[...]))
    lane = lax.broadcasted_iota(jnp.int32, logits.shape, 1)
    lane_f = lane.astype(F32)
    neg = -jnp.inf
    lg = jnp.where(lane < N_EXPERTS, logits, neg)
    m1 = jnp.max(lg, axis=1, keepdims=True)
    i1 = jnp.min(jnp.where(lg == m1, lane_f, float(ROUTE_LANES)), axis=1, keepdims=True)
    lg2 = jnp.where(lane_f == i1, neg, lg)
    m2 = jnp.max(lg2, axis=1, keepdims=True)
    i2 = jnp.min(jnp.where(lg2 == m2, lane_f, float(ROUTE_LANES)), axis=1, keepdims=True)
    e = jnp.exp(m2 - m1)
    w1 = 1.0 / (1.0 + e)
    w2 = e / (1.0 + e)
    route = jnp.where(lane == 0, i1, jnp.where(lane == 1, i2,
                      jnp.where(lane == 2, w1, jnp.where(lane == 3, w2, 0.0))))
    route_ref[...] = route


def _oproj_router(attn2d, x2d, wo, g, wr_hi, wr_lo):
    t, d = x2d.shape
    tm = min(TOKEN_TILE, t)
    tok = lambda i: (i, 0)
    return pl.pallas_call(
        _oproj_router_kernel,
        grid=(t // tm,),
        in_specs=[
            pl.BlockSpec((tm, d), tok),
            pl.BlockSpec((tm, d), tok),
            _const_spec((d, d)),
            _const_spec((1, d)),
            _const_spec((d, ROUTE_LANES)),
            _const_spec((d, ROUTE_LANES)),
        ],
        out_specs=[
            pl.BlockSpec((tm, d), tok),
            pl.BlockSpec((tm, d), tok),
            pl.BlockSpec((tm, ROUTE_LANES), tok),
        ],
        out_shape=[
            jax.ShapeDtypeStruct((t, d), F32),
            jax.ShapeDtypeStruct((t, d), BF16),
            jax.ShapeDtypeStruct((t, ROUTE_LANES), F32),
        ],
        compiler_params=_params(("parallel",)),
        name="oproj_router",
    )(attn2d, x2d, wo, g, wr_hi, wr_lo)


def _expert_kernel(te_ref, nt_ref, xs_ref, rw_ref, wg_ref, wu_ref, wd_ref, y_ref, *, f_chunks):
    i = pl.program_id(0)

    @pl.when(i < nt_ref[0])
    def _():
        xs = xs_ref[...]
        acc = jnp.zeros(y_ref.shape, F32)
        for lo, hi in f_chunks:
            a = _dot(xs, wg_ref[:, lo:hi])
            u = _dot(xs, wu_ref[:, lo:hi])
            acc = acc + _dot((_silu(a) * u).astype(BF16), wd_ref[lo:hi, :])
        y_ref[...] = (acc * rw_ref[...]).astype(y_ref.dtype)

    @pl.when(i >= nt_ref[0])
    def _():
        y_ref[...] = jnp.zeros_like(y_ref)


def _experts(tile_expert, n_tiles_used, xs, row_w, wg, wu, wd):
    p, d = xs.shape
    f = wg.shape[2]
    tm = EXPERT_TILE
    n_tiles = p // tm

    def row(i, te, nt):
        return (jnp.minimum(i, nt[0] - 1), 0)

    def wsel(i, te, nt):
        return (te[jnp.minimum(i, nt[0] - 1)], 0, 0)

    grid_spec = pltpu.PrefetchScalarGridSpec(
        num_scalar_prefetch=2,
        grid=(n_tiles,),
        in_specs=[
            pl.BlockSpec((tm, d), row),
            pl.BlockSpec((tm, 1), row),
            pl.BlockSpec((None, d, f), wsel, pipeline_mode=pl.Buffered(1)),
            pl.BlockSpec((None, d, f), wsel, pipeline_mode=pl.Buffered(1)),
            pl.BlockSpec((None, f, d), wsel, pipeline_mode=pl.Buffered(1)),
        ],
        out_specs=pl.BlockSpec((tm, d), lambda i, te, nt: (i, 0)),
    )
    return pl.pallas_call(
        functools.partial(_expert_kernel, f_chunks=_chunks(f, EXPERT_F_CHUNK)),
        grid_spec=grid_spec,
        out_shape=jax.ShapeDtypeStruct((p, d), BF16),
        compiler_params=_params(("arbitrary",)),
        name="experts",
    )(tile_expert, n_tiles_used, xs, row_w, wg, wu, wd)


def _route_tables(route, tm):
    t = route.shape[0]
    n_pairs = t * TOP_K
    n_tiles = n_pairs // tm + N_EXPERTS - 1
    p = n_tiles * tm
    e_pair = route[:, 0:TOP_K].astype(jnp.int32).reshape(n_pairs)
    w_pair = route[:, TOP_K:2 * TOP_K].reshape(n_pairs)
    onehot = (e_pair[:, None] == jnp.arange(N_EXPERTS, dtype=jnp.int32)[None, :]).astype(jnp.int32)
    counts = jnp.sum(onehot, axis=0)
    rank = jnp.sum((jnp.cumsum(onehot, axis=0) - onehot) * onehot, axis=1)
    tiles_per = (counts + tm - 1) // tm
    tile_end = jnp.cumsum(tiles_per)
    tile_start = tile_end - tiles_per
    dest = tile_start[e_pair] * tm + rank
    row_token = jnp.zeros((p,), jnp.int32).at[dest].set(
        jnp.arange(n_pairs, dtype=jnp.int32) // TOP_K, unique_indices=True)
    row_w = jnp.zeros((p,), F32).at[dest].set(w_pair, unique_indices=True)
    tile_ids = jnp.arange(n_tiles, dtype=jnp.int32)
    tile_expert = jnp.minimum(
        jnp.sum((tile_ids[:, None] >= tile_end[None, :]).astype(jnp.int32), axis=1),
        N_EXPERTS - 1)
    n_used = tile_end[-1:].astype(jnp.int32)
    return row_token, row_w.reshape(p, 1), tile_expert, n_used, dest.reshape(t, TOP_K)


def _final_kernel(x_ref, ya_ref, yb_ref, g_ref, o_ref):
    x = x_ref[...] + (ya_ref[...].astype(F32) + yb_ref[...].astype(F32))
    o_ref[...] = _rms(x, NORM_EPS) * g_ref[...]


def _final(x2d, ya, yb, g):
    t, d = x2d.shape
    tm = min(2 * TOKEN_TILE, t)
    tok = lambda i: (i, 0)
    return pl.pallas_call(
        _final_kernel,
        grid=(t // tm,),
        in_specs=[pl.BlockSpec((tm, d), tok)] * 3 + [_const_spec((1, d))],
        out_specs=pl.BlockSpec((tm, d), tok),
        out_shape=jax.ShapeDtypeStruct((t, d), F32),
        compiler_params=_params(("parallel",)),
        name="final_norm",
    )(x2d, ya, yb, g)


def _rope_angles(seq, dim):
    inv = 1.0 / (ROPE_THETA ** (jnp.arange(0, dim, 2, dtype=F32) / dim))
    return jnp.arange(seq, dtype=F32)[:, None] * inv[None, :]


def _retention_decays():
    h, c = RET_HEADS, RET_CHUNK
    log_gamma = jnp.log1p(-jnp.exp2(-5.0 - jnp.arange(h, dtype=F32)))
    pos = jnp.arange(c, dtype=F32)
    rel = pos[:, None] - pos[None, :]
    intra = jnp.where(rel >= 0, jnp.exp(log_gamma[:, None, None] * jnp.maximum(rel, 0.0)), 0.0)
    q_decay = jnp.exp(log_gamma[:, None] * (pos + 1.0))[:, :, None]
    k_decay = jnp.exp(log_gamma[:, None] * (c - 1.0 - pos))[:, :, None]
    chunk_decay = jnp.exp(log_gamma * c)
    return chunk_decay, intra, q_decay, k_decay


def kernel(x, attn_norm_g, ffn_norm_g, ret_w_in, ret_w_out, kv_norm_g, w_kv, diff_w_q,
           diff_lam_q1, diff_lam_k1, diff_lam_q2, diff_lam_k2, diff_subln_g, diff_w_o,
           ffn_w_gate, ffn_w_up, ffn_w_down, moe_w_router, moe_w_gate, moe_w_up, moe_w_down,
           final_norm_g):
    b, s, d = x.shape
    t = b * s
    row = lambda g: g.reshape(1, -1).astype(F32)
    bf = lambda w: w.astype(BF16)

    ang = _rope_angles(s, RET_QK_DIM)
    q, k, v, gate = _ret_proj(x.reshape(t, d), row(attn_norm_g[0]), bf(ret_w_in[0]),
                              jnp.cos(ang), jnp.sin(ang), s)
    chunk_decay, intra_decay, q_decay, k_decay = _retention_decays()
    x1 = _retention(chunk_decay, q.reshape(b, s, d), k.reshape(b, s, d),
                    v.reshape(b, s, 2 * d), gate.reshape(b, s, 2 * d), x,
                    bf(ret_w_out[0]), intra_decay, q_decay, k_decay)
    x2 = _ffn(x1.reshape(t, d), row(ffn_norm_g[0]), bf(ffn_w_gate[0]), bf(ffn_w_up[0]),
              bf(ffn_w_down[0]))

    ang = _rope_angles(s, DIFF_HEAD_DIM)
    ang = jnp.concatenate([ang] * (LANES // ang.shape[1]), axis=-1)
    first_half = (jnp.arange(LANES) % DIFF_HEAD_DIM) < DIFF_HEAD_DIM // 2
    sin = jnp.sin(ang)
    sin_a = jnp.where(first_half[None, :], -sin, 0.0)
    sin_b = jnp.where(first_half[None, :], 0.0, sin)
    ang_t = ang[:, :DIFF_HEAD_DIM // 2].T
    ks, vt, qt = _kvq_proj(x2.reshape(b, s, d), row(kv_norm_g), row(attn_norm_g[1]),
                           bf(w_kv[:, :d]), bf(w_kv[:, d:].T), bf(diff_w_q[0].T),
                           jnp.cos(ang), sin_a, sin_b, jnp.cos(ang_t), jnp.sin(ang_t))
    lambda_init = 0.8 - 0.6 * math.exp(-0.3 * 1)
    lam = (jnp.exp(jnp.sum(diff_lam_q1[0].astype(F32) * diff_lam_k1[0].astype(F32)))
           - jnp.exp(jnp.sum(diff_lam_q2[0].astype(F32) * diff_lam_k2[0].astype(F32)))
           + lambda_init).reshape(1)
    g_cols = jnp.broadcast_to(diff_subln_g[0].astype(F32)[:, None], (DIFF_V_DIM, vt.shape[-1]))
    attn = _diff_attention(lam, qt, ks, vt, g_cols, lambda_init)

    wr = jnp.zeros((d, ROUTE_LANES), F32).at[:, :N_EXPERTS].set(moe_w_router[0])
    wr_hi = wr.astype(BF16)
    wr_lo = (wr - wr_hi.astype(F32)).astype(BF16)
    x3, h, route = _oproj_router(attn.reshape(t, d), x2, bf(diff_w_o[0]), row(ffn_norm_g[1]),
                                 wr_hi, wr_lo)
    row_token, row_w, tile_expert, n_used, dest = _route_tables(route, EXPERT_TILE)
    xs = jnp.take(h, row_token, axis=0)
    y = _experts(tile_expert, n_used, xs, row_w, bf(moe_w_gate[0]), bf(moe_w_up[0]),
                 bf(moe_w_down[0]))
    ya = jnp.take(y, dest[:, 0], axis=0)
    yb = jnp.take(y, dest[:, 1], axis=0)
    out = _final(x3, ya, yb, row(final_norm_g))
    return out.reshape(b, s, d)
```

```python
import functools
import math

import jax
import jax.numpy as jnp
from jax import lax
from jax.experimental import pallas as pl
from jax.experimental.pallas import tpu as pltpu

F32 = jnp.float32
BF16 = jnp.bfloat16

D_MODEL = 1024
RET_HEADS = 4
RET_QK_DIM = D_MODEL // RET_HEADS
RET_V_DIM = 2 * D_MODEL // RET_HEADS
RET_CHUNK = 128
DIFF_HEADS = 8
DIFF_HEAD_DIM = D_MODEL // DIFF_HEADS // 2
DIFF_V_DIM = 2 * DIFF_HEAD_DIM
N_EXPERTS = 8
TOP_K = 2
ROPE_THETA = 10000.0
NORM_EPS = 1e-6
SUBLN_EPS = 1e-5
LOG2E = math.log2(math.e)

LANES = 128
VMEM_LIMIT = 56 * 1024 * 1024

TOKEN_TILE = 512
ATTN_TILE = 1024
EXPERT_TILE = 512
EXPERT_F_CHUNK = 512
ROUTE_LANES = LANES


def _dot(a, b):
    return jnp.dot(a, b, preferred_element_type=F32)


def _dot_nt(a, b):
    return lax.dot_general(a, b, (((1,), (1,)), ((), ())), preferred_element_type=F32)


def _dot_tn(a, b):
    return lax.dot_general(a, b, (((0,), (0,)), ((), ())), preferred_element_type=F32)


def _rms(x, eps):
    return x * lax.rsqrt(jnp.mean(x * x, axis=-1, keepdims=True) + eps)


def _silu(x):
    return x * (1.0 / (1.0 + jnp.exp(-x)))


def _const_spec(shape):
    zeros = (0,) * len(shape)
    return pl.BlockSpec(shape, lambda *_: zeros, pipeline_mode=pl.Buffered(1))


def _params(semantics):
    return pltpu.CompilerParams(dimension_semantics=semantics, vmem_limit_bytes=VMEM_LIMIT)


def _ret_proj_kernel(x_ref, g_ref, w_ref, cos_ref, sin_ref, q_ref, k_ref, v_ref, gate_ref):
    d = D_MODEL
    hb = (_rms(x_ref[...], NORM_EPS) * g_ref[...]).astype(BF16)
    c = cos_ref[...]
    s = sin_ref[...]
    half = RET_QK_DIM // 2
    for dst, col0, scale in ((q_ref, 0, 1.0), (k_ref, d, RET_QK_DIM ** -0.5)):
        y = _dot(hb, w_ref[:, col0:col0 + d])
        for h in range(RET_HEADS):
            lo = h * RET_QK_DIM
            y1 = y[:, lo:lo + half]
            y2 = y[:, lo + half:lo + 2 * half]
            dst[:, lo:lo + half] = ((y1 * c - y2 * s) * scale).astype(BF16)
            dst[:, lo + half:lo + 2 * half] = ((y2 * c + y1 * s) * scale).astype(BF16)
    for j in range(2):
        v_ref[:, j * d:(j + 1) * d] = _dot(hb, w_ref[:, (2 + j) * d:(3 + j) * d]).astype(BF16)
        gate_ref[:, j * d:(j + 1) * d] = _silu(
            _dot(hb, w_ref[:, (4 + j) * d:(5 + j) * d])).astype(BF16)


def _ret_proj(x2d, g, w_in, cos, sin, seq):
    t, d = x2d.shape
    tm = min(TOKEN_TILE, seq)
    n_seq = seq // tm
    tok = lambda i: (i, 0)
    pos = lambda i: (i % n_seq, 0)
    return pl.pallas_call(
        _ret_proj_kernel,
        grid=(t // tm,),
        in_specs=[
            pl.BlockSpec((tm, d), tok),
            _const_spec((1, d)),
            _const_spec((d, 6 * d)),
            pl.BlockSpec((tm, LANES), pos),
            pl.BlockSpec((tm, LANES), pos),
        ],
        out_specs=[
            pl.BlockSpec((tm, d), tok),
            pl.BlockSpec((tm, d), tok),
            pl.BlockSpec((tm, 2 * d), tok),
            pl.BlockSpec((tm, 2 * d), tok),
        ],
        out_shape=[
            jax.ShapeDtypeStruct((t, d), BF16),
            jax.ShapeDtypeStruct((t, d), BF16),
            jax.ShapeDtypeStruct((t, 2 * d), BF16),
            jax.ShapeDtypeStruct((t, 2 * d), BF16),
        ],
        compiler_params=_params(("parallel",)),
        name="ret_proj",
    )(x2d, g, w_in, cos, sin)


def _retention_kernel(cd_ref, q_ref, k_ref, v_ref, gate_ref, x_ref, wout_ref,
                      intra_ref, qdec_ref, kdec_ref, o_ref, state_ref, gated_ref):
    c = RET_CHUNK
    n_chunks = q_ref.shape[0] // c

    @pl.when(pl.program_id(1) == 0)
    def _():
        state_ref[...] = jnp.zeros_like(state_ref)

    for h in range(RET_HEADS):
        qk = slice(h * RET_QK_DIM, (h + 1) * RET_QK_DIM)
        vv = slice(h * RET_V_DIM, (h + 1) * RET_V_DIM)
        for n in range(n_chunks):
            rows = slice(n * c, (n + 1) * c)
            qi = q_ref[rows, qk]
            ki = k_ref[rows, qk]
            vi = v_ref[rows, vv]
            scores = _dot_nt(qi, ki) * intra_ref[h]
            intra = _dot(scores.astype(BF16), vi)
            state = state_ref[h]
            inter = _dot(qi, state.astype(BF16)) * qdec_ref[h]
            kd = (ki.astype(F32) * kdec_ref[h]).astype(BF16)
            state_ref[h] = state * cd_ref[h] + _dot_tn(kd, vi)
            o = _rms(intra + inter, NORM_EPS)
            gated_ref[rows, vv] = (gate_ref[rows, vv].astype(F32) * o).astype(BF16)
    o_ref[...] = x_ref[...] + _dot(gated_ref[...], wout_ref[...])


def _retention(chunk_decay, q, k, v, gate, x, w_out, intra_decay, q_decay, k_decay):
    b, s, d = x.shape
    ts = min(TOKEN_TILE, s)
    blk = lambda w: pl.BlockSpec((None, ts, w), lambda bi, si: (bi, si, 0))
    return pl.pallas_call(
        _retention_kernel,
        grid=(b, s // ts),
        in_specs=[
            pl.BlockSpec(memory_space=pltpu.SMEM),
            blk(d), blk(d), blk(2 * d), blk(2 * d), blk(d),
            _const_spec((2 * d, d)),
            _const_spec((RET_HEADS, RET_CHUNK, RET_CHUNK)),
            _const_spec((RET_HEADS, RET_CHUNK, 1)),
            _const_spec((RET_HEADS, RET_CHUNK, 1)),
        ],
        out_specs=blk(d),
        out_shape=jax.ShapeDtypeStruct((b, s, d), F32),
        scratch_shapes=[
            pltpu.VMEM((RET_HEADS, RET_QK_DIM, RET_V_DIM), F32),
            pltpu.VMEM((ts, 2 * d), BF16),
        ],
        compiler_params=_params(("parallel", "arbitrary")),
        name="retention",
    )(chunk_decay, q, k, v, gate, x, w_out, intra_decay, q_decay, k_decay)


def _ffn_kernel(x_ref, g_ref, wg_ref, wu_ref, wd_ref, o_ref, *, f_chunks):
    x = x_ref[...]
    hb = (_rms(x, NORM_EPS) * g_ref[...]).astype(BF16)
    acc = x
    for lo, hi in f_chunks:
        a = _dot(hb, wg_ref[:, lo:hi])
        u = _dot(hb, wu_ref[:, lo:hi])
        acc = acc + _dot((_silu(a) * u).astype(BF16), wd_ref[lo:hi, :])
    o_ref[...] = acc


def _chunks(total, size):
    return tuple((lo, min(lo + size, total)) for lo in range(0, total, size))


def _ffn(x2d, g, wg, wu, wd):
    t, d = x2d.shape
    f = wg.shape[1]
    tm = min(TOKEN_TILE, t)
    tok = lambda i: (i, 0)
    return pl.pallas_call(
        functools.partial(_ffn_kernel, f_chunks=_chunks(f, 1024)),
        grid=(t // tm,),
        in_specs=[
            pl.BlockSpec((tm, d), tok),
            _const_spec((1, d)),
            _const_spec((d, f)),
            _const_spec((d, f)),
            _const_spec((f, d)),
        ],
        out_specs=pl.BlockSpec((tm, d), tok),
        out_shape=jax.ShapeDtypeStruct((t, d), F32),
        compiler_params=_params(("parallel",)),
        name="dense_ffn",
    )(x2d, g, wg, wu, wd)


def _rope64(y, cos, sin_a, sin_b):
    half = DIFF_HEAD_DIM // 2
    return (y * cos + pltpu.roll(y, LANES - half, 1) * sin_a
            + pltpu.roll(y, half, 1) * sin_b)


def _kvq_proj_kernel(x_ref, gkv_ref, gq_ref, wk_ref, wvt_ref, wqt_ref, cos_ref, sa_ref, sb_ref,
                     cost_ref, sint_ref, k_ref, vt_ref, qt_ref):
    d = D_MODEL
    xn = _rms(x_ref[...], NORM_EPS)
    hkv = (xn * gkv_ref[...]).astype(BF16)
    hq = (xn * gq_ref[...]).astype(BF16)
    cos = cos_ref[...]
    sa = sa_ref[...]
    sb = sb_ref[...]
    yk = _dot(hkv, wk_ref[...])
    for j in range(d // LANES):
        cols = slice(j * LANES, (j + 1) * LANES)
        k_ref[:, cols] = _rope64(yk[:, cols], cos, sa, sb).astype(BF16)

    yvt = _dot_nt(wvt_ref[...], hkv)
    for h in range(DIFF_HEADS):
        vt_ref[h] = yvt[h * DIFF_V_DIM:(h + 1) * DIFF_V_DIM, :].astype(BF16)

    yqt = _dot_nt(wqt_ref[...], hq)
    ct = cost_ref[...]
    st = sint_ref[...]
    half = DIFF_HEAD_DIM // 2
    q_scale = DIFF_HEAD_DIM ** -0.5 * LOG2E
    for g in range(d // DIFF_HEAD_DIM):
        r0 = g * DIFF_HEAD_DIM
        a = yqt[r0:r0 + half, :]
        b = yqt[r0 + half:r0 + 2 * half, :]
        qt_ref[r0:r0 + half, :] = ((a * ct - b * st) * q_scale).astype(BF16)
        qt_ref[r0 + half:r0 + 2 * half, :] = ((b * ct + a * st) * q_scale).astype(BF16)


def _kvq_proj(x, gkv, gq, wk, wvt, wqt, cos, sin_a, sin_b, cos_t, sin_t):
    b, s, d = x.shape
    tm = min(ATTN_TILE, s)
    n_seq = s // tm
    half = DIFF_HEAD_DIM // 2
    pos = lambda bi, i: (i, 0)
    pos_t = lambda bi, i: (0, i)
    return pl.pallas_call(
        _kvq_proj_kernel,
        grid=(b, n_seq),
        in_specs=[
            pl.BlockSpec((None, tm, d), lambda bi, i: (bi, i, 0)),
            _const_spec((1, d)),
            _const_spec((1, d)),
            _const_spec((d, d)),
            _const_spec((d, d)),
            _const_spec((d, d)),
            pl.BlockSpec((tm, LANES), pos),
            pl.BlockSpec((tm, LANES), pos),
            pl.BlockSpec((tm, LANES), pos),
            pl.BlockSpec((half, tm), pos_t),
            pl.BlockSpec((half, tm), pos_t),
        ],
        out_specs=[
            pl.BlockSpec((None, tm, d), lambda bi, i: (bi, i, 0)),
            pl.BlockSpec((None, DIFF_HEADS, None, DIFF_V_DIM, tm), lambda bi, i: (bi, 0, i, 0, 0)),
            pl.BlockSpec((None, d, tm), lambda bi, i: (bi, 0, i)),
        ],
        out_shape=[
            jax.ShapeDtypeStruct((b, s, d), BF16),
            jax.ShapeDtypeStruct((b, DIFF_HEADS, n_seq, DIFF_V_DIM, tm), BF16),
            jax.ShapeDtypeStruct((b, d, s), BF16),
        ],
        compiler_params=_params(("parallel", "parallel")),
        name="kvq_proj",
    )(x, gkv, gq, wk, wvt, wqt, cos, sin_a, sin_b, cos_t, sin_t)


MXU_COLS = 512
MASKED_LOGIT = -1e30


def _attn_kernel(lam_ref, qt_ref, k_ref, vt_ref, g_ref, o_ref, qq_ref, m_ref, l_ref, acc_ref,
                 s0_ref, *, out_scale):
    tq = qt_ref.shape[1]
    tk = vt_ref.shape[2]
    assert tq == tk
    r = 2 * tq
    qi = pl.program_id(2)

    qt = qt_ref[...]
    feat = lax.broadcasted_iota(jnp.int32, qt.shape, 0)
    zero = jnp.zeros_like(qt)
    qq_ref[:, 0:tq] = jnp.where(feat < DIFF_HEAD_DIM, qt, zero)
    qq_ref[:, tq:r] = jnp.where(feat >= DIFF_HEAD_DIM, qt, zero)
    m_ref[...] = jnp.full_like(m_ref, MASKED_LOGIT)
    l_ref[...] = jnp.zeros_like(l_ref)
    acc_ref[...] = jnp.zeros_like(acc_ref)

    n_groups = r // MXU_COLS

    def logits(j, c):
        start = pl.multiple_of(j * tk, tk)
        return _dot(k_ref[pl.ds(start, tk), :],
                    qq_ref[:, c * MXU_COLS:(c + 1) * MXU_COLS])

    def block(j, is_diagonal):
        vt = vt_ref[j]
        s_next = s0_ref[...]
        for c in range(n_groups):
            cols = slice(c * MXU_COLS, (c + 1) * MXU_COLS)
            s = s_next
            if c + 1 < n_groups:
                s_next = logits(j, c + 1)
            elif not is_diagonal:
                s0_ref[...] = logits(j + 1, 0)
            if is_diagonal:
                key = lax.broadcasted_iota(jnp.int32, s.shape, 0)
                qry = lax.broadcasted_iota(jnp.int32, s.shape, 1) + (c * MXU_COLS) % tq
                s = jnp.where(key <= qry, s, MASKED_LOGIT)
            m_prev = m_ref[:, cols]
            m_new = jnp.maximum(m_prev, jnp.max(s, axis=0, keepdims=True))
            alpha = jnp.exp2(m_prev - m_new)
            p = jnp.exp2(s - m_new)
            l_ref[:, cols] = alpha * l_ref[:, cols] + jnp.sum(p, axis=0, keepdims=True)
            acc_ref[:, cols] = alpha * acc_ref[:, cols] + _dot(vt, p.astype(BF16))
            m_ref[:, cols] = m_new

    def body(j, carry):
        block(j, False)
        return carry

    s0_ref[...] = logits(0, 0)
    lax.fori_loop(0, qi, body, 0)
    block(qi, True)

    o = acc_ref[...] * (1.0 / l_ref[...])
    out = o[:, 0:tq] - lam_ref[0] * o[:, tq:r]
    ms = jnp.mean(out * out, axis=0, keepdims=True)
    out = out * lax.rsqrt(ms + SUBLN_EPS) * g_ref[...] * out_scale
    o_ref[...] = out.T.astype(BF16)


def _diff_attention(lam, qt, k, vt, subln_g_cols, lambda_init):
    b, s, d = k.shape
    tq = vt.shape[-1]
    n_blk = s // tq
    return pl.pallas_call(
        functools.partial(_attn_kernel, out_scale=1.0 - lambda_init),
        grid=(b, DIFF_HEADS, n_blk),
        in_specs=[
            pl.BlockSpec(memory_space=pltpu.SMEM),
            pl.BlockSpec((None, LANES, tq), lambda bi, h, i: (bi, h, i)),
            pl.BlockSpec((None, s, LANES), lambda bi, h, i: (bi, 0, h)),
            pl.BlockSpec((None, None, n_blk, DIFF_V_DIM, tq), lambda bi, h, i: (bi, h, 0, 0, 0)),
            _const_spec((DIFF_V_DIM, tq)),
        ],
        out_specs=pl.BlockSpec((None, tq, LANES), lambda bi, h, i: (bi, i, h)),
        out_shape=jax.ShapeDtypeStruct((b, s, d), BF16),
        scratch_shapes=[
            pltpu.VMEM((LANES, 2 * tq), BF16),
            pltpu.VMEM((1, 2 * tq), F32),
            pltpu.VMEM((1, 2 * tq), F32),
            pltpu.VMEM((DIFF_V_DIM, 2 * tq), F32),
            pltpu.VMEM((tq, MXU_COLS), F32),
        ],
        compiler_params=_params(("parallel", "parallel", "arbitrary")),
        name="diff_attention",
    )(lam, qt, k, vt, subln_g_cols)


def _split_bf16(x):
    hi = x.astype(BF16)
    lo = (x - hi.astype(F32)).astype(BF16)
    return hi, lo


def _oproj_router_kernel(a_ref, x_ref, wo_ref, g_ref, wr_hi_ref, wr_lo_ref,
                         x_out_ref, h_ref, route_ref):
    x = x_ref[...] + _dot(a_ref[...], wo_ref[...])
    x_out_ref[...] = x
    h = _rms(x, NORM_EPS) * g_ref[...]
    h_hi, h_lo = _split_bf16(h)
    h_ref[...] = h_hi
    w_hi = wr_hi_ref[...]
    logits = _dot(h_hi, w_hi) + (_dot(h_lo, w_hi) + _dot(h_hi, wr_lo_ref[...]))
    lane = lax.broadcasted_iota(jnp.int32, logits.shape, 1)
    lane_f = lane.astype(F32)
    neg = -jnp.inf
    lg = jnp.where(lane < N_EXPERTS, logits, neg)
    m1 = jnp.max(lg, axis=1, keepdims=True)
    i1 = jnp.min(jnp.where(lg == m1, lane_f, float(ROUTE_LANES)), axis=1, keepdims=True)
    lg2 = jnp.where(lane_f == i1, neg, lg)
    m2 = jnp.max(lg2, axis=1, keepdims=True)
    i2 = jnp.min(jnp.where(lg2 == m2, lane_f, float(ROUTE_LANES)), axis=1, keepdims=True)
    e = jnp.exp(m2 - m1)
    w1 = 1.0 / (1.0 + e)
    w2 = e / (1.0 + e)
    route = jnp.where(lane == 0, i1, jnp.where(lane == 1, i2,
                      jnp.where(lane == 2, w1, jnp.where(lane == 3, w2, 0.0))))
    route_ref[...] = route


def _oproj_router(attn2d, x2d, wo, g, wr_hi, wr_lo):
    t, d = x2d.shape
    tm = min(TOKEN_TILE, t)
    tok = lambda i: (i, 0)
    return pl.pallas_call(
        _oproj_router_kernel,
        grid=(t // tm,),
        in_specs=[
            pl.BlockSpec((tm, d), tok),
            pl.BlockSpec((tm, d), tok),
            _const_spec((d, d)),
            _const_spec((1, d)),
            _const_spec((d, ROUTE_LANES)),
            _const_spec((d, ROUTE_LANES)),
        ],
        out_specs=[
            pl.BlockSpec((tm, d), tok),
            pl.BlockSpec((tm, d), tok),
            pl.BlockSpec((tm, ROUTE_LANES), tok),
        ],
        out_shape=[
            jax.ShapeDtypeStruct((t, d), F32),
            jax.ShapeDtypeStruct((t, d), BF16),
            jax.ShapeDtypeStruct((t, ROUTE_LANES), F32),
        ],
        compiler_params=_params(("parallel",)),
        name="oproj_router",
    )(attn2d, x2d, wo, g, wr_hi, wr_lo)


def _expert_kernel(te_ref, nt_ref, xs_ref, rw_ref, wg_ref, wu_ref, wd_ref, y_ref, *, f_chunks):
    i = pl.program_id(0)

    @pl.when(i < nt_ref[0])
    def _():
        xs = xs_ref[...]
        acc = jnp.zeros(y_ref.shape, F32)
        for lo, hi in f_chunks:
            a = _dot(xs, wg_ref[:, lo:hi])
            u = _dot(xs, wu_ref[:, lo:hi])
            acc = acc + _dot((_silu(a) * u).astype(BF16), wd_ref[lo:hi, :])
        y_ref[...] = (acc * rw_ref[...]).astype(y_ref.dtype)

    @pl.when(i >= nt_ref[0])
    def _():
        y_ref[...] = jnp.zeros_like(y_ref)


def _experts(tile_expert, n_tiles_used, xs, row_w, wg, wu, wd):
    p, d = xs.shape
    f = wg.shape[2]
    tm = EXPERT_TILE
    n_tiles = p // tm

    def row(i, te, nt):
        return (jnp.minimum(i, nt[0] - 1), 0)

    def wsel(i, te, nt):
        return (te[jnp.minimum(i, nt[0] - 1)], 0, 0)

    grid_spec = pltpu.PrefetchScalarGridSpec(
        num_scalar_prefetch=2,
        grid=(n_tiles,),
        in_specs=[
            pl.BlockSpec((tm, d), row),
            pl.BlockSpec((tm, 1), row),
            pl.BlockSpec((None, d, f), wsel, pipeline_mode=pl.Buffered(1)),
            pl.BlockSpec((None, d, f), wsel, pipeline_mode=pl.Buffered(1)),
            pl.BlockSpec((None, f, d), wsel, pipeline_mode=pl.Buffered(1)),
        ],
        out_specs=pl.BlockSpec((tm, d), lambda i, te, nt: (i, 0)),
    )
    return pl.pallas_call(
        functools.partial(_expert_kernel, f_chunks=_chunks(f, EXPERT_F_CHUNK)),
        grid_spec=grid_spec,
        out_shape=jax.ShapeDtypeStruct((p, d), BF16),
        compiler_params=_params(("arbitrary",)),
        name="experts",
    )(tile_expert, n_tiles_used, xs, row_w, wg, wu, wd)


def _route_tables(route, tm):
    t = route.shape[0]
    n_pairs = t * TOP_K
    n_tiles = n_pairs // tm + N_EXPERTS - 1
    p = n_tiles * tm
    e_pair = route[:, 0:TOP_K].astype(jnp.int32).reshape(n_pairs)
    w_pair = route[:, TOP_K:2 * TOP_K].reshape(n_pairs)
    onehot = (e_pair[:, None] == jnp.arange(N_EXPERTS, dtype=jnp.int32)[None, :]).astype(jnp.int32)
    counts = jnp.sum(onehot, axis=0)
    rank = jnp.sum((jnp.cumsum(onehot, axis=0) - onehot) * onehot, axis=1)
    tiles_per = (counts + tm - 1) // tm
    tile_end = jnp.cumsum(tiles_per)
    tile_start = tile_end - tiles_per
    dest = tile_start[e_pair] * tm + rank
    row_pair = jnp.full((p,), n_pairs, jnp.int32).at[dest].set(
        jnp.arange(n_pairs, dtype=jnp.int32), unique_indices=True)
    row_token = jnp.where(row_pair < n_pairs, row_pair // TOP_K, 0)
    row_w = jnp.concatenate([w_pair, jnp.zeros((1,), F32)])[row_pair]
    tile_ids = jnp.arange(n_tiles, dtype=jnp.int32)
    tile_expert = jnp.minimum(
        jnp.sum((tile_ids[:, None] >= tile_end[None, :]).astype(jnp.int32), axis=1),
        N_EXPERTS - 1)
    n_used = tile_end[-1:].astype(jnp.int32)
    return row_token, row_w.reshape(p, 1), tile_expert, n_used, dest.reshape(t, TOP_K)


def _final_kernel(x_ref, ya_ref, yb_ref, g_ref, o_ref):
    x = x_ref[...] + (ya_ref[...].astype(F32) + yb_ref[...].astype(F32))
    o_ref[...] = _rms(x, NORM_EPS) * g_ref[...]


def _final(x2d, ya, yb, g):
    t, d = x2d.shape
    tm = min(2 * TOKEN_TILE, t)
    tok = lambda i: (i, 0)
    return pl.pallas_call(
        _final_kernel,
        grid=(t // tm,),
        in_specs=[pl.BlockSpec((tm, d), tok)] * 3 + [_const_spec((1, d))],
        out_specs=pl.BlockSpec((tm, d), tok),
        out_shape=jax.ShapeDtypeStruct((t, d), F32),
        compiler_params=_params(("parallel",)),
        name="final_norm",
    )(x2d, ya, yb, g)


def _rope_angles(seq, dim):
    inv = 1.0 / (ROPE_THETA ** (jnp.arange(0, dim, 2, dtype=F32) / dim))
    return jnp.arange(seq, dtype=F32)[:, None] * inv[None, :]


def _retention_decays():
    h, c = RET_HEADS, RET_CHUNK
    log_gamma = jnp.log1p(-jnp.exp2(-5.0 - jnp.arange(h, dtype=F32)))
    pos = jnp.arange(c, dtype=F32)
    rel = pos[:, None] - pos[None, :]
    intra = jnp.where(rel >= 0, jnp.exp(log_gamma[:, None, None] * jnp.maximum(rel, 0.0)), 0.0)
    q_decay = jnp.exp(log_gamma[:, None] * (pos + 1.0))[:, :, None]
    k_decay = jnp.exp(log_gamma[:, None] * (c - 1.0 - pos))[:, :, None]
    chunk_decay = jnp.exp(log_gamma * c)
    return chunk_decay, intra, q_decay, k_decay


def kernel(x, attn_norm_g, ffn_norm_g, ret_w_in, ret_w_out, kv_norm_g, w_kv, diff_w_q,
           diff_lam_q1, diff_lam_k1, diff_lam_q2, diff_lam_k2, diff_subln_g, diff_w_o,
           ffn_w_gate, ffn_w_up, ffn_w_down, moe_w_router, moe_w_gate, moe_w_up, moe_w_down,
           final_norm_g):
    b, s, d = x.shape
    t = b * s
    row = lambda g: g.reshape(1, -1).astype(F32)
    bf = lambda w: w.astype(BF16)

    ang = _rope_angles(s, RET_QK_DIM)
    q, k, v, gate = _ret_proj(x.reshape(t, d), row(attn_norm_g[0]), bf(ret_w_in[0]),
                              jnp.cos(ang), jnp.sin(ang), s)
    chunk_decay, intra_decay, q_decay, k_decay = _retention_decays()
    x1 = _retention(chunk_decay, q.reshape(b, s, d), k.reshape(b, s, d),
                    v.reshape(b, s, 2 * d), gate.reshape(b, s, 2 * d), x,
                    bf(ret_w_out[0]), intra_decay, q_decay, k_decay)
    x2 = _ffn(x1.reshape(t, d), row(ffn_norm_g[0]), bf(ffn_w_gate[0]), bf(ffn_w_up[0]),
              bf(ffn_w_down[0]))

    ang = _rope_angles(s, DIFF_HEAD_DIM)
    cos32 = jnp.cos(ang)
    sin32 = jnp.sin(ang)
    zero32 = jnp.zeros_like(sin32)
    pairs = LANES // DIFF_HEAD_DIM
    cos = jnp.concatenate([cos32, cos32] * pairs, axis=-1)
    sin_a = jnp.concatenate([-sin32, zero32] * pairs, axis=-1)
    sin_b = jnp.concatenate([zero32, sin32] * pairs, axis=-1)
    ks, vt, qt = _kvq_proj(x2.reshape(b, s, d), row(kv_norm_g), row(attn_norm_g[1]),
                           bf(w_kv[:, :d]), bf(w_kv[:, d:].T), bf(diff_w_q[0].T),
                           cos, sin_a, sin_b, cos32.T, sin32.T)
    lambda_init = 0.8 - 0.6 * math.exp(-0.3 * 1)
    lam = (jnp.exp(jnp.sum(diff_lam_q1[0].astype(F32) * diff_lam_k1[0].astype(F32)))
           - jnp.exp(jnp.sum(diff_lam_q2[0].astype(F32) * diff_lam_k2[0].astype(F32)))
           + lambda_init).reshape(1)
    g_cols = jnp.broadcast_to(diff_subln_g[0].astype(F32)[:, None], (DIFF_V_DIM, vt.shape[-1]))
    attn = _diff_attention(lam, qt, ks, vt, g_cols, lambda_init)

    wr = jnp.zeros((d, ROUTE_LANES), F32).at[:, :N_EXPERTS].set(moe_w_router[0])
    wr_hi = wr.astype(BF16)
    wr_lo = (wr - wr_hi.astype(F32)).astype(BF16)
    x3, h, route = _oproj_router(attn.reshape(t, d), x2, bf(diff_w_o[0]), row(ffn_norm_g[1]),
                                 wr_hi, wr_lo)
    row_token, row_w, tile_expert, n_used, dest = _route_tables(route, EXPERT_TILE)
    xs = jnp.take(h, row_token, axis=0)
    y = _experts(tile_expert, n_used, xs, row_w, bf(moe_w_gate[0]), bf(moe_w_up[0]),
                 bf(moe_w_down[0]))
    ya = jnp.take(y, dest[:, 0], axis=0)
    yb = jnp.take(y, dest[:, 1], axis=0)
    out = _final(x3, ya, yb, row(final_norm_g))
    return out.reshape(b, s, d)
```

```python
import functools
import math

import jax
import jax.numpy as jnp
from jax import lax
from jax.experimental import pallas as pl
from jax.experimental.pallas import tpu as pltpu

F32 = jnp.float32
BF16 = jnp.bfloat16

D_MODEL = 1024
RET_HEADS = 4
RET_QK_DIM = D_MODEL // RET_HEADS
RET_V_DIM = 2 * D_MODEL // RET_HEADS
RET_CHUNK = 128
DIFF_HEADS = 8
DIFF_HEAD_DIM = D_MODEL // DIFF_HEADS // 2
DIFF_V_DIM = 2 * DIFF_HEAD_DIM
N_EXPERTS = 8
TOP_K = 2
ROPE_THETA = 10000.0
NORM_EPS = 1e-6
SUBLN_EPS = 1e-5
LOG2E = math.log2(math.e)

LANES = 128
VMEM_LIMIT = 56 * 1024 * 1024

TOKEN_TILE = 512
ATTN_TILE = 1024
EXPERT_TILE = 512
EXPERT_F_CHUNK = 512
ROUTE_LANES = LANES


def _dot(a, b):
    return jnp.dot(a, b, preferred_element_type=F32)


def _dot_nt(a, b):
    return lax.dot_general(a, b, (((1,), (1,)), ((), ())), preferred_element_type=F32)


def _dot_tn(a, b):
    return lax.dot_general(a, b, (((0,), (0,)), ((), ())), preferred_element_type=F32)


def _rms(x, eps):
    return x * lax.rsqrt(jnp.mean(x * x, axis=-1, keepdims=True) + eps)


def _silu(x):
    return x * (1.0 / (1.0 + jnp.exp(-x)))


def _const_spec(shape):
    zeros = (0,) * len(shape)
    return pl.BlockSpec(shape, lambda *_: zeros, pipeline_mode=pl.Buffered(1))


def _params(semantics):
    return pltpu.CompilerParams(dimension_semantics=semantics, vmem_limit_bytes=VMEM_LIMIT)


def _ret_proj_kernel(x_ref, g_ref, w_ref, cos_ref, sin_ref, q_ref, k_ref, v_ref, gate_ref):
    d = D_MODEL
    hb = (_rms(x_ref[...], NORM_EPS) * g_ref[...]).astype(BF16)
    c = cos_ref[...]
    s = sin_ref[...]
    half = RET_QK_DIM // 2
    for dst, col0, scale in ((q_ref, 0, 1.0), (k_ref, d, RET_QK_DIM ** -0.5)):
        y = _dot(hb, w_ref[:, col0:col0 + d])
        for h in range(RET_HEADS):
            lo = h * RET_QK_DIM
            y1 = y[:, lo:lo + half]
            y2 = y[:, lo + half:lo + 2 * half]
            dst[:, lo:lo + half] = ((y1 * c - y2 * s) * scale).astype(BF16)
            dst[:, lo + half:lo + 2 * half] = ((y2 * c + y1 * s) * scale).astype(BF16)
    for j in range(2):
        v_ref[:, j * d:(j + 1) * d] = _dot(hb, w_ref[:, (2 + j) * d:(3 + j) * d]).astype(BF16)
        gate_ref[:, j * d:(j + 1) * d] = _silu(
            _dot(hb, w_ref[:, (4 + j) * d:(5 + j) * d])).astype(BF16)


def _ret_proj(x2d, g, w_in, cos, sin, seq):
    t, d = x2d.shape
    tm = min(TOKEN_TILE, seq)
    n_seq = seq // tm
    tok = lambda i: (i, 0)
    pos = lambda i: (i % n_seq, 0)
    return pl.pallas_call(
        _ret_proj_kernel,
        grid=(t // tm,),
        in_specs=[
            pl.BlockSpec((tm, d), tok),
            _const_spec((1, d)),
            _const_spec((d, 6 * d)),
            pl.BlockSpec((tm, LANES), pos),
            pl.BlockSpec((tm, LANES), pos),
        ],
        out_specs=[
            pl.BlockSpec((tm, d), tok),
            pl.BlockSpec((tm, d), tok),
            pl.BlockSpec((tm, 2 * d), tok),
            pl.BlockSpec((tm, 2 * d), tok),
        ],
        out_shape=[
            jax.ShapeDtypeStruct((t, d), BF16),
            jax.ShapeDtypeStruct((t, d), BF16),
            jax.ShapeDtypeStruct((t, 2 * d), BF16),
            jax.ShapeDtypeStruct((t, 2 * d), BF16),
        ],
        compiler_params=_params(("parallel",)),
        name="ret_proj",
    )(x2d, g, w_in, cos, sin)


def _retention_kernel(cd_ref, q_ref, k_ref, v_ref, gate_ref, x_ref, wout_ref,
                      intra_ref, qdec_ref, kdec_ref, o_ref, state_ref, gated_ref):
    c = RET_CHUNK
    n_chunks = q_ref.shape[0] // c

    @pl.when(pl.program_id(1) == 0)
    def _():
        state_ref[...] = jnp.zeros_like(state_ref)

    for h in range(RET_HEADS):
        qk = slice(h * RET_QK_DIM, (h + 1) * RET_QK_DIM)
        vv = slice(h * RET_V_DIM, (h + 1) * RET_V_DIM)
        for n in range(n_chunks):
            rows = slice(n * c, (n + 1) * c)
            qi = q_ref[rows, qk]
            ki = k_ref[rows, qk]
            vi = v_ref[rows, vv]
            scores = _dot_nt(qi, ki) * intra_ref[h]
            intra = _dot(scores.astype(BF16), vi)
            state = state_ref[h]
            inter = _dot(qi, state.astype(BF16)) * qdec_ref[h]
            kd = (ki.astype(F32) * kdec_ref[h]).astype(BF16)
            state_ref[h] = state * cd_ref[h] + _dot_tn(kd, vi)
            o = _rms(intra + inter, NORM_EPS)
            gated_ref[rows, vv] = (gate_ref[rows, vv].astype(F32) * o).astype(BF16)
    o_ref[...] = x_ref[...] + _dot(gated_ref[...], wout_ref[...])


def _retention(chunk_decay, q, k, v, gate, x, w_out, intra_decay, q_decay, k_decay):
    b, s, d = x.shape
    ts = min(TOKEN_TILE, s)
    blk = lambda w: pl.BlockSpec((None, ts, w), lambda bi, si: (bi, si, 0))
    return pl.pallas_call(
        _retention_kernel,
        grid=(b, s // ts),
        in_specs=[
            pl.BlockSpec(memory_space=pltpu.SMEM),
            blk(d), blk(d), blk(2 * d), blk(2 * d), blk(d),
            _const_spec((2 * d, d)),
            _const_spec((RET_HEADS, RET_CHUNK, RET_CHUNK)),
            _const_spec((RET_HEADS, RET_CHUNK, 1)),
            _const_spec((RET_HEADS, RET_CHUNK, 1)),
        ],
        out_specs=blk(d),
        out_shape=jax.ShapeDtypeStruct((b, s, d), F32),
        scratch_shapes=[
            pltpu.VMEM((RET_HEADS, RET_QK_DIM, RET_V_DIM), F32),
            pltpu.VMEM((ts, 2 * d), BF16),
        ],
        compiler_params=_params(("parallel", "arbitrary")),
        name="retention",
    )(chunk_decay, q, k, v, gate, x, w_out, intra_decay, q_decay, k_decay)


def _ffn_kernel(x_ref, g_ref, wg_ref, wu_ref, wd_ref, o_ref, *, f_chunks):
    x = x_ref[...]
    hb = (_rms(x, NORM_EPS) * g_ref[...]).astype(BF16)
    acc = x
    for lo, hi in f_chunks:
        a = _dot(hb, wg_ref[:, lo:hi])
        u = _dot(hb, wu_ref[:, lo:hi])
        acc = acc + _dot((_silu(a) * u).astype(BF16), wd_ref[lo:hi, :])
    o_ref[...] = acc


def _chunks(total, size):
    return tuple((lo, min(lo + size, total)) for lo in range(0, total, size))


def _ffn(x2d, g, wg, wu, wd):
    t, d = x2d.shape
    f = wg.shape[1]
    tm = min(TOKEN_TILE, t)
    tok = lambda i: (i, 0)
    return pl.pallas_call(
        functools.partial(_ffn_kernel, f_chunks=_chunks(f, 1024)),
        grid=(t // tm,),
        in_specs=[
            pl.BlockSpec((tm, d), tok),
            _const_spec((1, d)),
            _const_spec((d, f)),
            _const_spec((d, f)),
            _const_spec((f, d)),
        ],
        out_specs=pl.BlockSpec((tm, d), tok),
        out_shape=jax.ShapeDtypeStruct((t, d), F32),
        compiler_params=_params(("parallel",)),
        name="dense_ffn",
    )(x2d, g, wg, wu, wd)


def _rope64(y, cos, sin_a, sin_b):
    half = DIFF_HEAD_DIM // 2
    return (y * cos + pltpu.roll(y, LANES - half, 1) * sin_a
            + pltpu.roll(y, half, 1) * sin_b)


def _kvq_proj_kernel(x_ref, gkv_ref, gq_ref, wk_ref, wvt_ref, wqt_ref, cos_ref, sa_ref, sb_ref,
                     cost_ref, sint_ref, k_ref, vt_ref, qt_ref):
    d = D_MODEL
    xn = _rms(x_ref[...], NORM_EPS)
    hkv = (xn * gkv_ref[...]).astype(BF16)
    hq = (xn * gq_ref[...]).astype(BF16)
    cos = cos_ref[...]
    sa = sa_ref[...]
    sb = sb_ref[...]
    yk = _dot(hkv, wk_ref[...])
    for j in range(d // LANES):
        cols = slice(j * LANES, (j + 1) * LANES)
        k_ref[:, cols] = _rope64(yk[:, cols], cos, sa, sb).astype(BF16)

    yvt = _dot_nt(wvt_ref[...], hkv)
    for h in range(DIFF_HEADS):
        vt_ref[h] = yvt[h * DIFF_V_DIM:(h + 1) * DIFF_V_DIM, :].astype(BF16)

    yqt = _dot_nt(wqt_ref[...], hq)
    ct = cost_ref[...]
    st = sint_ref[...]
    half = DIFF_HEAD_DIM // 2
    q_scale = DIFF_HEAD_DIM ** -0.5 * LOG2E
    for g in range(d // DIFF_HEAD_DIM):
        r0 = g * DIFF_HEAD_DIM
        a = yqt[r0:r0 + half, :]
        b = yqt[r0 + half:r0 + 2 * half, :]
        qt_ref[r0:r0 + half, :] = ((a * ct - b * st) * q_scale).astype(BF16)
        qt_ref[r0 + half:r0 + 2 * half, :] = ((b * ct + a * st) * q_scale).astype(BF16)


def _kvq_proj(x, gkv, gq, wk, wvt, wqt, cos, sin_a, sin_b, cos_t, sin_t):
    b, s, d = x.shape
    tm = min(ATTN_TILE, s)
    n_seq = s // tm
    half = DIFF_HEAD_DIM // 2
    pos = lambda bi, i: (i, 0)
    pos_t = lambda bi, i: (0, i)
    return pl.pallas_call(
        _kvq_proj_kernel,
        grid=(b, n_seq),
        in_specs=[
            pl.BlockSpec((None, tm, d), lambda bi, i: (bi, i, 0)),
            _const_spec((1, d)),
            _const_spec((1, d)),
            _const_spec((d, d)),
            _const_spec((d, d)),
            _const_spec((d, d)),
            pl.BlockSpec((tm, LANES), pos),
            pl.BlockSpec((tm, LANES), pos),
            pl.BlockSpec((tm, LANES), pos),
            pl.BlockSpec((half, tm), pos_t),
            pl.BlockSpec((half, tm), pos_t),
        ],
        out_specs=[
            pl.BlockSpec((None, tm, d), lambda bi, i: (bi, i, 0)),
            pl.BlockSpec((None, DIFF_HEADS, None, DIFF_V_DIM, tm), lambda bi, i: (bi, 0, i, 0, 0)),
            pl.BlockSpec((None, d, tm), lambda bi, i: (bi, 0, i)),
        ],
        out_shape=[
            jax.ShapeDtypeStruct((b, s, d), BF16),
            jax.ShapeDtypeStruct((b, DIFF_HEADS, n_seq, DIFF_V_DIM, tm), BF16),
            jax.ShapeDtypeStruct((b, d, s), BF16),
        ],
        compiler_params=_params(("parallel", "parallel")),
        name="kvq_proj",
    )(x, gkv, gq, wk, wvt, wqt, cos, sin_a, sin_b, cos_t, sin_t)


MXU_COLS = 512
MASKED_LOGIT = -1e30
LOGITS_AHEAD = 1


def _attn_kernel(lam_ref, qt_ref, k_ref, vt_ref, g_ref, o_ref, qq_ref, m_ref, l_ref, acc_ref,
                 s_ref, *, out_scale):
    tq = qt_ref.shape[1]
    tk = vt_ref.shape[2]
    assert tq == tk
    r = 2 * tq
    qi = pl.program_id(2)

    qt = qt_ref[...]
    feat = lax.broadcasted_iota(jnp.int32, qt.shape, 0)
    zero = jnp.zeros_like(qt)
    qq_ref[:, 0:tq] = jnp.where(feat < DIFF_HEAD_DIM, qt, zero)
    qq_ref[:, tq:r] = jnp.where(feat >= DIFF_HEAD_DIM, qt, zero)
    m_ref[...] = jnp.full_like(m_ref, MASKED_LOGIT)
    l_ref[...] = jnp.zeros_like(l_ref)
    acc_ref[...] = jnp.zeros_like(acc_ref)

    n_groups = r // MXU_COLS
    ahead = s_ref.shape[0]
    assert n_groups >= ahead

    def keys_needed(c):
        return min(tk, (c * MXU_COLS) % tq + MXU_COLS)

    def logits(j, c, n_keys=tk):
        start = pl.multiple_of(j * tk, tk)
        return _dot(k_ref[pl.ds(start, n_keys), :],
                    qq_ref[:, c * MXU_COLS:(c + 1) * MXU_COLS])

    def block(j, is_diagonal):
        vt = vt_ref[j]
        pending = [s_ref[a] for a in range(ahead)]
        for c in range(n_groups):
            cols = slice(c * MXU_COLS, (c + 1) * MXU_COLS)
            s = pending.pop(0)
            if c + ahead < n_groups:
                n_next = keys_needed(c + ahead) if is_diagonal else tk
                pending.append(logits(j, c + ahead, n_next))
            elif not is_diagonal:
                s_ref[c + ahead - n_groups] = logits(j + 1, c + ahead - n_groups)
            vt_c = vt
            if is_diagonal:
                n_keys = keys_needed(c)
                s = s[0:n_keys, :]
                vt_c = vt[:, 0:n_keys]
                key = lax.broadcasted_iota(jnp.int32, s.shape, 0)
                qry = lax.broadcasted_iota(jnp.int32, s.shape, 1) + (c * MXU_COLS) % tq
                s = jnp.where(key <= qry, s, MASKED_LOGIT)
            m_prev = m_ref[:, cols]
            m_new = jnp.maximum(m_prev, jnp.max(s, axis=0, keepdims=True))
            alpha = jnp.exp2(m_prev - m_new)
            p = jnp.exp2(s - m_new)
            l_ref[:, cols] = alpha * l_ref[:, cols] + jnp.sum(p, axis=0, keepdims=True)
            acc_ref[:, cols] = alpha * acc_ref[:, cols] + _dot(vt_c, p.astype(BF16))
            m_ref[:, cols] = m_new

    def body(j, carry):
        block(j, False)
        return carry

    for a in range(ahead):
        s_ref[a] = logits(0, a)
    lax.fori_loop(0, qi, body, 0)
    block(qi, True)

    o = acc_ref[...] * (1.0 / l_ref[...])
    out = o[:, 0:tq] - lam_ref[0] * o[:, tq:r]
    ms = jnp.mean(out * out, axis=0, keepdims=True)
    out = out * lax.rsqrt(ms + SUBLN_EPS) * g_ref[...] * out_scale
    o_ref[...] = out.T.astype(BF16)


def _diff_attention(lam, qt, k, vt, subln_g_cols, lambda_init):
    b, s, d = k.shape
    tq = vt.shape[-1]
    n_blk = s // tq
    return pl.pallas_call(
        functools.partial(_attn_kernel, out_scale=1.0 - lambda_init),
        grid=(b, DIFF_HEADS, n_blk),
        in_specs=[
            pl.BlockSpec(memory_space=pltpu.SMEM),
            pl.BlockSpec((None, LANES, tq), lambda bi, h, i: (bi, h, i)),
            pl.BlockSpec((None, s, LANES), lambda bi, h, i: (bi, 0, h)),
            pl.BlockSpec((None, None, n_blk, DIFF_V_DIM, tq), lambda bi, h, i: (bi, h, 0, 0, 0)),
            _const_spec((DIFF_V_DIM, tq)),
        ],
        out_specs=pl.BlockSpec((None, tq, LANES), lambda bi, h, i: (bi, i, h)),
        out_shape=jax.ShapeDtypeStruct((b, s, d), BF16),
        scratch_shapes=[
            pltpu.VMEM((LANES, 2 * tq), BF16),
            pltpu.VMEM((1, 2 * tq), F32),
            pltpu.VMEM((1, 2 * tq), F32),
            pltpu.VMEM((DIFF_V_DIM, 2 * tq), F32),
            pltpu.VMEM((LOGITS_AHEAD, tq, MXU_COLS), F32),
        ],
        compiler_params=_params(("parallel", "parallel", "arbitrary")),
        name="diff_attention",
    )(lam, qt, k, vt, subln_g_cols)


def _split_bf16(x):
    hi = x.astype(BF16)
    lo = (x - hi.astype(F32)).astype(BF16)
    return hi, lo


def _oproj_router_kernel(a_ref, x_ref, wo_ref, g_ref, wr_hi_ref, wr_lo_ref,
                         x_out_ref, h_ref, route_ref):
    x = x_ref[...] + _dot(a_ref[...], wo_ref[...])
    x_out_ref[...] = x
    h = _rms(x, NORM_EPS) * g_ref[...]
    h_hi, h_lo = _split_bf16(h)
    h_ref[...] = h_hi
    w_hi = wr_hi_ref[...]
    logits = _dot(h_hi, w_hi) + (_dot(h_lo, w_hi) + _dot(h_hi, wr_lo_ref[...]))
    lane = lax.broadcasted_iota(jnp.int32, logits.shape, 1)
    lane_f = lane.astype(F32)
    neg = -jnp.inf
    lg = jnp.where(lane < N_EXPERTS, logits, neg)
    m1 = jnp.max(lg, axis=1, keepdims=True)
    i1 = jnp.min(jnp.where(lg == m1, lane_f, float(ROUTE_LANES)), axis=1, keepdims=True)
    lg2 = jnp.where(lane_f == i1, neg, lg)
    m2 = jnp.max(lg2, axis=1, keepdims=True)
    i2 = jnp.min(jnp.where(lg2 == m2, lane_f, float(ROUTE_LANES)), axis=1, keepdims=True)
    e = jnp.exp(m2 - m1)
    w1 = 1.0 / (1.0 + e)
    w2 = e / (1.0 + e)
    route = jnp.where(lane == 0, i1, jnp.where(lane == 1, i2,
                      jnp.where(lane == 2, w1, jnp.where(lane == 3, w2, 0.0))))
    route_ref[...] = route


def _oproj_router(attn2d, x2d, wo, g, wr_hi, wr_lo):
    t, d = x2d.shape
    tm = min(TOKEN_TILE, t)
    tok = lambda i: (i, 0)
    return pl.pallas_call(
        _oproj_router_kernel,
        grid=(t // tm,),
        in_specs=[
            pl.BlockSpec((tm, d), tok),
            pl.BlockSpec((tm, d), tok),
            _const_spec((d, d)),
            _const_spec((1, d)),
            _const_spec((d, ROUTE_LANES)),
            _const_spec((d, ROUTE_LANES)),
        ],
        out_specs=[
            pl.BlockSpec((tm, d), tok),
            pl.BlockSpec((tm, d), tok),
            pl.BlockSpec((tm, ROUTE_LANES), tok),
        ],
        out_shape=[
            jax.ShapeDtypeStruct((t, d), F32),
            jax.ShapeDtypeStruct((t, d), BF16),
            jax.ShapeDtypeStruct((t, ROUTE_LANES), F32),
        ],
        compiler_params=_params(("parallel",)),
        name="oproj_router",
    )(attn2d, x2d, wo, g, wr_hi, wr_lo)


WEIGHT_SLOTS = 2


def _expert_kernel(te_ref, nt_ref, xs_ref, rw_ref, wg_hbm, wu_hbm, wd_hbm, y_ref,
                   wg_ref, wu_ref, wd_ref, sg_ref, su_ref, sd_ref, sem, *, fc):
    i = pl.program_id(0)
    e = te_ref[i]
    active = i < nt_ref[0]
    first = jnp.logical_and(active, jnp.logical_or(i == 0, e != te_ref[jnp.maximum(i - 1, 0)]))
    n_chunks = wg_ref.shape[1] // fc

    def chunk_copies(c):
        slot = c % WEIGHT_SLOTS
        cols = pl.ds(c * fc, fc)
        return (
            pltpu.make_async_copy(wg_hbm.at[e, :, cols], sg_ref.at[slot], sem.at[0, slot]),
            pltpu.make_async_copy(wu_hbm.at[e, :, cols], su_ref.at[slot], sem.at[1, slot]),
            pltpu.make_async_copy(wd_hbm.at[e, cols, :], sd_ref.at[slot], sem.at[2, slot]),
        )

    def swiglu_chunk(xs, c):
        lo, hi = c * fc, (c + 1) * fc
        a = _dot(xs, wg_ref[:, lo:hi])
        u = _dot(xs, wu_ref[:, lo:hi])
        return _dot((_silu(a) * u).astype(BF16), wd_ref[lo:hi, :])

    def finish(acc):
        y_ref[...] = (acc * rw_ref[...]).astype(y_ref.dtype)

    @pl.when(first)
    def _():
        for c in range(min(WEIGHT_SLOTS, n_chunks)):
            for cp in chunk_copies(c):
                cp.start()
        xs = xs_ref[...]
        acc = jnp.zeros(y_ref.shape, F32)
        for c in range(n_chunks):
            slot = c % WEIGHT_SLOTS
            lo, hi = c * fc, (c + 1) * fc
            for cp in chunk_copies(c):
                cp.wait()
            wg_ref[:, lo:hi] = sg_ref[slot].astype(BF16)
            wu_ref[:, lo:hi] = su_ref[slot].astype(BF16)
            wd_ref[lo:hi, :] = sd_ref[slot].astype(BF16)
            if c + WEIGHT_SLOTS < n_chunks:
                for cp in chunk_copies(c + WEIGHT_SLOTS):
                    cp.start()
            acc = acc + swiglu_chunk(xs, c)
        finish(acc)

    @pl.when(jnp.logical_and(active, jnp.logical_not(first)))
    def _():
        xs = xs_ref[...]
        acc = jnp.zeros(y_ref.shape, F32)
        for c in range(n_chunks):
            acc = acc + swiglu_chunk(xs, c)
        finish(acc)

    @pl.when(jnp.logical_not(active))
    def _():
        y_ref[...] = jnp.zeros_like(y_ref)


def _experts(tile_expert, n_tiles_used, xs, row_w, wg, wu, wd):
    p, d = xs.shape
    f = wg.shape[2]
    tm = EXPERT_TILE
    fc = EXPERT_F_CHUNK
    assert f % fc == 0
    n_tiles = p // tm

    def row(i, te, nt):
        return (jnp.minimum(i, nt[0] - 1), 0)

    grid_spec = pltpu.PrefetchScalarGridSpec(
        num_scalar_prefetch=2,
        grid=(n_tiles,),
        in_specs=[
            pl.BlockSpec((tm, d), row),
            pl.BlockSpec((tm, 1), row),
            pl.BlockSpec(memory_space=pl.ANY),
            pl.BlockSpec(memory_space=pl.ANY),
            pl.BlockSpec(memory_space=pl.ANY),
        ],
        out_specs=pl.BlockSpec((tm, d), lambda i, te, nt: (i, 0)),
        scratch_shapes=[
            pltpu.VMEM((d, f), BF16),
            pltpu.VMEM((d, f), BF16),
            pltpu.VMEM((f, d), BF16),
            pltpu.VMEM((WEIGHT_SLOTS, d, fc), F32),
            pltpu.VMEM((WEIGHT_SLOTS, d, fc), F32),
            pltpu.VMEM((WEIGHT_SLOTS, fc, d), F32),
            pltpu.SemaphoreType.DMA((3, WEIGHT_SLOTS)),
        ],
    )
    return pl.pallas_call(
        functools.partial(_expert_kernel, fc=fc),
        grid_spec=grid_spec,
        out_shape=jax.ShapeDtypeStruct((p, d), BF16),
        compiler_params=_params(("arbitrary",)),
        name="experts",
    )(tile_expert, n_tiles_used, xs, row_w, wg, wu, wd)


def _route_tables(route, tm):
    t = route.shape[0]
    n_pairs = t * TOP_K
    n_tiles = n_pairs // tm + N_EXPERTS - 1
    p = n_tiles * tm
    e_pair = route[:, 0:TOP_K].astype(jnp.int32).reshape(n_pairs)
    w_pair = route[:, TOP_K:2 * TOP_K].reshape(n_pairs)
    onehot = (e_pair[:, None] == jnp.arange(N_EXPERTS, dtype=jnp.int32)[None, :]).astype(jnp.int32)
    counts = jnp.sum(onehot, axis=0)
    rank = jnp.sum((jnp.cumsum(onehot, axis=0) - onehot) * onehot, axis=1)
    tiles_per = (counts + tm - 1) // tm
    tile_end = jnp.cumsum(tiles_per)
    tile_start = tile_end - tiles_per
    dest = tile_start[e_pair] * tm + rank
    row_pair = jnp.full((p,), n_pairs, jnp.int32).at[dest].set(
        jnp.arange(n_pairs, dtype=jnp.int32), unique_indices=True)
    row_token = jnp.where(row_pair < n_pairs, row_pair // TOP_K, 0)
    row_w = jnp.concatenate([w_pair, jnp.zeros((1,), F32)])[row_pair]
    tile_ids = jnp.arange(n_tiles, dtype=jnp.int32)
    tile_expert = jnp.minimum(
        jnp.sum((tile_ids[:, None] >= tile_end[None, :]).astype(jnp.int32), axis=1),
        N_EXPERTS - 1)
    n_used = tile_end[-1:].astype(jnp.int32)
    return row_token, row_w.reshape(p, 1), tile_expert, n_used, dest.reshape(t, TOP_K)


def _final_kernel(x_ref, ya_ref, yb_ref, g_ref, o_ref):
    x = x_ref[...] + (ya_ref[...].astype(F32) + yb_ref[...].astype(F32))
    o_ref[...] = _rms(x, NORM_EPS) * g_ref[...]


def _final(x2d, ya, yb, g):
    t, d = x2d.shape
    tm = min(2 * TOKEN_TILE, t)
    tok = lambda i: (i, 0)
    return pl.pallas_call(
        _final_kernel,
        grid=(t // tm,),
        in_specs=[pl.BlockSpec((tm, d), tok)] * 3 + [_const_spec((1, d))],
        out_specs=pl.BlockSpec((tm, d), tok),
        out_shape=jax.ShapeDtypeStruct((t, d), F32),
        compiler_params=_params(("parallel",)),
        name="final_norm",
    )(x2d, ya, yb, g)


def _rope_angles(seq, dim):
    inv = 1.0 / (ROPE_THETA ** (jnp.arange(0, dim, 2, dtype=F32) / dim))
    return jnp.arange(seq, dtype=F32)[:, None] * inv[None, :]


def _retention_decays():
    h, c = RET_HEADS, RET_CHUNK
    log_gamma = jnp.log1p(-jnp.exp2(-5.0 - jnp.arange(h, dtype=F32)))
    pos = jnp.arange(c, dtype=F32)
    rel = pos[:, None] - pos[None, :]
    intra = jnp.where(rel >= 0, jnp.exp(log_gamma[:, None, None] * jnp.maximum(rel, 0.0)), 0.0)
    q_decay = jnp.exp(log_gamma[:, None] * (pos + 1.0))[:, :, None]
    k_decay = jnp.exp(log_gamma[:, None] * (c - 1.0 - pos))[:, :, None]
    chunk_decay = jnp.exp(log_gamma * c)
    return chunk_decay, intra, q_decay, k_decay


def kernel(x, attn_norm_g, ffn_norm_g, ret_w_in, ret_w_out, kv_norm_g, w_kv, diff_w_q,
           diff_lam_q1, diff_lam_k1, diff_lam_q2, diff_lam_k2, diff_subln_g, diff_w_o,
           ffn_w_gate, ffn_w_up, ffn_w_down, moe_w_router, moe_w_gate, moe_w_up, moe_w_down,
           final_norm_g):
    b, s, d = x.shape
    t = b * s
    row = lambda g: g.reshape(1, -1).astype(F32)
    bf = lambda w: w.astype(BF16)

    ang = _rope_angles(s, RET_QK_DIM)
    q, k, v, gate = _ret_proj(x.reshape(t, d), row(attn_norm_g[0]), bf(ret_w_in[0]),
                              jnp.cos(ang), jnp.sin(ang), s)
    chunk_decay, intra_decay, q_decay, k_decay = _retention_decays()
    x1 = _retention(chunk_decay, q.reshape(b, s, d), k.reshape(b, s, d),
                    v.reshape(b, s, 2 * d), gate.reshape(b, s, 2 * d), x,
                    bf(ret_w_out[0]), intra_decay, q_decay, k_decay)
    x2 = _ffn(x1.reshape(t, d), row(ffn_norm_g[0]), bf(ffn_w_gate[0]), bf(ffn_w_up[0]),
              bf(ffn_w_down[0]))

    ang = _rope_angles(s, DIFF_HEAD_DIM)
    cos32, sin32 = lax.optimization_barrier((jnp.cos(ang), jnp.sin(ang)))
    zero32 = jnp.zeros_like(sin32)
    pairs = LANES // DIFF_HEAD_DIM
    cos = jnp.concatenate([cos32, cos32] * pairs, axis=-1)
    sin_a = jnp.concatenate([-sin32, zero32] * pairs, axis=-1)
    sin_b = jnp.concatenate([zero32, sin32] * pairs, axis=-1)
    ks, vt, qt = _kvq_proj(x2.reshape(b, s, d), row(kv_norm_g), row(attn_norm_g[1]),
                           bf(w_kv[:, :d]), bf(w_kv[:, d:].T), bf(diff_w_q[0].T),
                           cos, sin_a, sin_b, cos32.T, sin32.T)
    lambda_init = 0.8 - 0.6 * math.exp(-0.3 * 1)
    lam = (jnp.exp(jnp.sum(diff_lam_q1[0].astype(F32) * diff_lam_k1[0].astype(F32)))
           - jnp.exp(jnp.sum(diff_lam_q2[0].astype(F32) * diff_lam_k2[0].astype(F32)))
           + lambda_init).reshape(1)
    g_cols = jnp.broadcast_to(diff_subln_g[0].astype(F32)[:, None], (DIFF_V_DIM, vt.shape[-1]))
    attn = _diff_attention(lam, qt, ks, vt, g_cols, lambda_init)

    wr = jnp.zeros((d, ROUTE_LANES), F32).at[:, :N_EXPERTS].set(moe_w_router[0])
    wr_hi = wr.astype(BF16)
    wr_lo = (wr - wr_hi.astype(F32)).astype(BF16)
    x3, h, route = _oproj_router(attn.reshape(t, d), x2, bf(diff_w_o[0]), row(ffn_norm_g[1]),
                                 wr_hi, wr_lo)
    row_token, row_w, tile_expert, n_used, dest = _route_tables(route, EXPERT_TILE)
    xs = jnp.take(h, row_token, axis=0)
    y = _experts(tile_expert, n_used, xs, row_w, moe_w_gate[0], moe_w_up[0], moe_w_down[0])
    ya = jnp.take(y, dest[:, 0], axis=0)
    yb = jnp.take(y, dest[:, 1], axis=0)
    out = _final(x3, ya, yb, row(final_norm_g))
    return out.reshape(b, s, d)
```

```python
import functools
import math

import jax
import jax.numpy as jnp
from jax import lax
from jax.experimental import pallas as pl
from jax.experimental.pallas import tpu as pltpu

F32 = jnp.float32
BF16 = jnp.bfloat16

D_MODEL = 1024
RET_HEADS = 4
RET_QK_DIM = D_MODEL // RET_HEADS
RET_V_DIM = 2 * D_MODEL // RET_HEADS
RET_CHUNK = 256
DIFF_HEADS = 8
DIFF_HEAD_DIM = D_MODEL // DIFF_HEADS // 2
DIFF_V_DIM = 2 * DIFF_HEAD_DIM
N_EXPERTS = 8
TOP_K = 2
ROPE_THETA = 10000.0
NORM_EPS = 1e-6
SUBLN_EPS = 1e-5
LOG2E = math.log2(math.e)

LANES = 128
VMEM_LIMIT = 56 * 1024 * 1024

TOKEN_TILE = 512
ATTN_TILE = 2048
EXPERT_TILE = 512
EXPERT_F_CHUNK = 512
ROUTE_LANES = LANES


def _dot(a, b):
    return jnp.dot(a, b, preferred_element_type=F32)


def _dot_nt(a, b):
    return lax.dot_general(a, b, (((1,), (1,)), ((), ())), preferred_element_type=F32)


def _dot_tn(a, b):
    return lax.dot_general(a, b, (((0,), (0,)), ((), ())), preferred_element_type=F32)


def _rms(x, eps):
    return x * lax.rsqrt(jnp.mean(x * x, axis=-1, keepdims=True) + eps)


def _silu(x):
    return x * (1.0 / (1.0 + jnp.exp(-x)))


def _const_spec(shape):
    zeros = (0,) * len(shape)
    return pl.BlockSpec(shape, lambda *_: zeros, pipeline_mode=pl.Buffered(1))


def _params(semantics):
    return pltpu.CompilerParams(dimension_semantics=semantics, vmem_limit_bytes=VMEM_LIMIT)


def _ret_proj_kernel(x_ref, g_ref, w_ref, cos_ref, sin_ref, q_ref, k_ref, v_ref, gate_ref):
    d = D_MODEL
    hb = (_rms(x_ref[...], NORM_EPS) * g_ref[...]).astype(BF16)
    c = cos_ref[...]
    s = sin_ref[...]
    half = RET_QK_DIM // 2
    for dst, col0, scale in ((q_ref, 0, 1.0), (k_ref, d, RET_QK_DIM ** -0.5)):
        y = _dot(hb, w_ref[:, col0:col0 + d])
        for h in range(RET_HEADS):
            lo = h * RET_QK_DIM
            y1 = y[:, lo:lo + half]
            y2 = y[:, lo + half:lo + 2 * half]
            dst[:, lo:lo + half] = ((y1 * c - y2 * s) * scale).astype(BF16)
            dst[:, lo + half:lo + 2 * half] = ((y2 * c + y1 * s) * scale).astype(BF16)
    for j in range(2):
        v_ref[:, j * d:(j + 1) * d] = _dot(hb, w_ref[:, (2 + j) * d:(3 + j) * d]).astype(BF16)
        gate_ref[:, j * d:(j + 1) * d] = _silu(
            _dot(hb, w_ref[:, (4 + j) * d:(5 + j) * d])).astype(BF16)


def _ret_proj(x2d, g, w_in, cos, sin, seq):
    t, d = x2d.shape
    tm = min(TOKEN_TILE, seq)
    n_seq = seq // tm
    tok = lambda i: (i, 0)
    pos = lambda i: (i % n_seq, 0)
    return pl.pallas_call(
        _ret_proj_kernel,
        grid=(t // tm,),
        in_specs=[
            pl.BlockSpec((tm, d), tok),
            _const_spec((1, d)),
            _const_spec((d, 6 * d)),
            pl.BlockSpec((tm, LANES), pos),
            pl.BlockSpec((tm, LANES), pos),
        ],
        out_specs=[
            pl.BlockSpec((tm, d), tok),
            pl.BlockSpec((tm, d), tok),
            pl.BlockSpec((tm, 2 * d), tok),
            pl.BlockSpec((tm, 2 * d), tok),
        ],
        out_shape=[
            jax.ShapeDtypeStruct((t, d), BF16),
            jax.ShapeDtypeStruct((t, d), BF16),
            jax.ShapeDtypeStruct((t, 2 * d), BF16),
            jax.ShapeDtypeStruct((t, 2 * d), BF16),
        ],
        compiler_params=_params(("parallel",)),
        name="ret_proj",
    )(x2d, g, w_in, cos, sin)


def _retention_kernel(cd_ref, q_ref, k_ref, v_ref, gate_ref, x_ref, wout_ref,
                      intra_ref, qdec_ref, kdec_ref, o_ref, state_ref, gated_ref):
    c = RET_CHUNK
    n_chunks = q_ref.shape[0] // c

    @pl.when(pl.program_id(1) == 0)
    def _():
        state_ref[...] = jnp.zeros_like(state_ref)

    for h in range(RET_HEADS):
        qk = slice(h * RET_QK_DIM, (h + 1) * RET_QK_DIM)
        vv = slice(h * RET_V_DIM, (h + 1) * RET_V_DIM)
        for n in range(n_chunks):
            rows = slice(n * c, (n + 1) * c)
            qi = q_ref[rows, qk]
            ki = k_ref[rows, qk]
            vi = v_ref[rows, vv]
            scores = _dot_nt(qi, ki) * intra_ref[h]
            intra = _dot(scores.astype(BF16), vi)
            state = state_ref[h]
            inter = _dot(qi, state.astype(BF16)) * qdec_ref[h]
            kd = (ki.astype(F32) * kdec_ref[h]).astype(BF16)
            state_ref[h] = state * cd_ref[h] + _dot_tn(kd, vi)
            o = _rms(intra + inter, NORM_EPS)
            gated_ref[rows, vv] = (gate_ref[rows, vv].astype(F32) * o).astype(BF16)
    o_ref[...] = x_ref[...] + _dot(gated_ref[...], wout_ref[...])


def _retention(chunk_decay, q, k, v, gate, x, w_out, intra_decay, q_decay, k_decay):
    b, s, d = x.shape
    ts = min(TOKEN_TILE, s)
    blk = lambda w: pl.BlockSpec((None, ts, w), lambda bi, si: (bi, si, 0))
    return pl.pallas_call(
        _retention_kernel,
        grid=(b, s // ts),
        in_specs=[
            pl.BlockSpec(memory_space=pltpu.SMEM),
            blk(d), blk(d), blk(2 * d), blk(2 * d), blk(d),
            _const_spec((2 * d, d)),
            _const_spec((RET_HEADS, RET_CHUNK, RET_CHUNK)),
            _const_spec((RET_HEADS, RET_CHUNK, 1)),
            _const_spec((RET_HEADS, RET_CHUNK, 1)),
        ],
        out_specs=blk(d),
        out_shape=jax.ShapeDtypeStruct((b, s, d), F32),
        scratch_shapes=[
            pltpu.VMEM((RET_HEADS, RET_QK_DIM, RET_V_DIM), F32),
            pltpu.VMEM((ts, 2 * d), BF16),
        ],
        compiler_params=_params(("parallel", "arbitrary")),
        name="retention",
    )(chunk_decay, q, k, v, gate, x, w_out, intra_decay, q_decay, k_decay)


def _ffn_kernel(x_ref, g_ref, wg_ref, wu_ref, wd_ref, o_ref, *, f_chunks):
    x = x_ref[...]
    hb = (_rms(x, NORM_EPS) * g_ref[...]).astype(BF16)
    acc = x
    for lo, hi in f_chunks:
        a = _dot(hb, wg_ref[:, lo:hi])
        u = _dot(hb, wu_ref[:, lo:hi])
        acc = acc + _dot((_silu(a) * u).astype(BF16), wd_ref[lo:hi, :])
    o_ref[...] = acc


def _chunks(total, size):
    return tuple((lo, min(lo + size, total)) for lo in range(0, total, size))


def _ffn(x2d, g, wg, wu, wd):
    t, d = x2d.shape
    f = wg.shape[1]
    tm = min(TOKEN_TILE, t)
    tok = lambda i: (i, 0)
    return pl.pallas_call(
        functools.partial(_ffn_kernel, f_chunks=_chunks(f, 1024)),
        grid=(t // tm,),
        in_specs=[
            pl.BlockSpec((tm, d), tok),
            _const_spec((1, d)),
            _const_spec((d, f)),
            _const_spec((d, f)),
            _const_spec((f, d)),
        ],
        out_specs=pl.BlockSpec((tm, d), tok),
        out_shape=jax.ShapeDtypeStruct((t, d), F32),
        compiler_params=_params(("parallel",)),
        name="dense_ffn",
    )(x2d, g, wg, wu, wd)


def _rope64(y, cos, sin_a, sin_b):
    half = DIFF_HEAD_DIM // 2
    return (y * cos + pltpu.roll(y, LANES - half, 1) * sin_a
            + pltpu.roll(y, half, 1) * sin_b)


def _kvq_proj_kernel(x_ref, gkv_ref, gq_ref, wk_ref, wvt_ref, wqt_ref, cos_ref, sa_ref, sb_ref,
                     cost_ref, sint_ref, k_ref, vt_ref, qt_ref):
    d = D_MODEL
    xn = _rms(x_ref[...], NORM_EPS)
    hkv = (xn * gkv_ref[...]).astype(BF16)
    hq = (xn * gq_ref[...]).astype(BF16)
    cos = cos_ref[...]
    sa = sa_ref[...]
    sb = sb_ref[...]
    yk = _dot(hkv, wk_ref[...])
    for j in range(d // LANES):
        cols = slice(j * LANES, (j + 1) * LANES)
        k_ref[:, cols] = _rope64(yk[:, cols], cos, sa, sb).astype(BF16)

    yvt = _dot_nt(wvt_ref[...], hkv)
    for h in range(DIFF_HEADS):
        vt_ref[h] = yvt[h * DIFF_V_DIM:(h + 1) * DIFF_V_DIM, :].astype(BF16)

    yqt = _dot_nt(wqt_ref[...], hq)
    ct = cost_ref[...]
    st = sint_ref[...]
    half = DIFF_HEAD_DIM // 2
    q_scale = DIFF_HEAD_DIM ** -0.5 * LOG2E
    for g in range(d // DIFF_HEAD_DIM):
        r0 = g * DIFF_HEAD_DIM
        a = yqt[r0:r0 + half, :]
        b = yqt[r0 + half:r0 + 2 * half, :]
        qt_ref[r0:r0 + half, :] = ((a * ct - b * st) * q_scale).astype(BF16)
        qt_ref[r0 + half:r0 + 2 * half, :] = ((b * ct + a * st) * q_scale).astype(BF16)


def _kvq_proj(x, gkv, gq, wk, wvt, wqt, cos, sin_a, sin_b, cos_t, sin_t):
    b, s, d = x.shape
    tm = min(TOKEN_TILE, s)
    n_seq = s // tm
    tk = min(ATTN_TILE, s)
    per_blk = tk // tm
    half = DIFF_HEAD_DIM // 2
    pos = lambda bi, i: (i, 0)
    pos_t = lambda bi, i: (0, i)
    return pl.pallas_call(
        _kvq_proj_kernel,
        grid=(b, n_seq),
        in_specs=[
            pl.BlockSpec((None, tm, d), lambda bi, i: (bi, i, 0)),
            _const_spec((1, d)),
            _const_spec((1, d)),
            _const_spec((d, d)),
            _const_spec((d, d)),
            _const_spec((d, d)),
            pl.BlockSpec((tm, LANES), pos),
            pl.BlockSpec((tm, LANES), pos),
            pl.BlockSpec((tm, LANES), pos),
            pl.BlockSpec((half, tm), pos_t),
            pl.BlockSpec((half, tm), pos_t),
        ],
        out_specs=[
            pl.BlockSpec((None, tm, d), lambda bi, i: (bi, i, 0)),
            pl.BlockSpec((None, DIFF_HEADS, None, DIFF_V_DIM, tm),
                         lambda bi, i: (bi, 0, i // per_blk, 0, i % per_blk)),
            pl.BlockSpec((None, d, tm), lambda bi, i: (bi, 0, i)),
        ],
        out_shape=[
            jax.ShapeDtypeStruct((b, s, d), BF16),
            jax.ShapeDtypeStruct((b, DIFF_HEADS, s // tk, DIFF_V_DIM, tk), BF16),
            jax.ShapeDtypeStruct((b, d, s), BF16),
        ],
        compiler_params=_params(("parallel", "parallel")),
        name="kvq_proj",
    )(x, gkv, gq, wk, wvt, wqt, cos, sin_a, sin_b, cos_t, sin_t)


MXU_COLS = 512
MASKED_LOGIT = -1e30
LOGITS_AHEAD = 1


def _attn_kernel(lam_ref, qt_ref, k_ref, vt_ref, g_ref, o_ref, qq_ref, m_ref, l_ref, acc_ref,
                 s_ref, *, out_scale):
    tq = qt_ref.shape[1]
    tk = vt_ref.shape[2]
    assert tq == tk
    r = 2 * tq
    qi = pl.program_id(2)

    qt = qt_ref[...]
    feat = lax.broadcasted_iota(jnp.int32, qt.shape, 0)
    zero = jnp.zeros_like(qt)
    qq_ref[:, 0:tq] = jnp.where(feat < DIFF_HEAD_DIM, qt, zero)
    qq_ref[:, tq:r] = jnp.where(feat >= DIFF_HEAD_DIM, qt, zero)
    m_ref[...] = jnp.full_like(m_ref, MASKED_LOGIT)
    l_ref[...] = jnp.zeros_like(l_ref)
    acc_ref[...] = jnp.zeros_like(acc_ref)

    n_groups = r // MXU_COLS
    ahead = s_ref.shape[0]
    assert n_groups >= ahead

    def keys_needed(c):
        return min(tk, (c * MXU_COLS) % tq + MXU_COLS)

    def logits(j, c, n_keys=tk):
        start = pl.multiple_of(j * tk, tk)
        return _dot(k_ref[pl.ds(start, n_keys), :],
                    qq_ref[:, c * MXU_COLS:(c + 1) * MXU_COLS])

    def block(j, is_diagonal):
        vt = vt_ref[j]
        pending = [s_ref[a] for a in range(ahead)]
        for c in range(n_groups):
            cols = slice(c * MXU_COLS, (c + 1) * MXU_COLS)
            s = pending.pop(0)
            if c + ahead < n_groups:
                n_next = keys_needed(c + ahead) if is_diagonal else tk
                pending.append(logits(j, c + ahead, n_next))
            elif not is_diagonal:
                s_ref[c + ahead - n_groups] = logits(j + 1, c + ahead - n_groups)
            vt_c = vt
            if is_diagonal:
                n_keys = keys_needed(c)
                s = s[0:n_keys, :]
                vt_c = vt[:, 0:n_keys]
                key = lax.broadcasted_iota(jnp.int32, s.shape, 0)
                qry = lax.broadcasted_iota(jnp.int32, s.shape, 1) + (c * MXU_COLS) % tq
                s = jnp.where(key <= qry, s, MASKED_LOGIT)
            m_prev = m_ref[:, cols]
            m_new = jnp.maximum(m_prev, jnp.max(s, axis=0, keepdims=True))
            alpha = jnp.exp2(m_prev - m_new)
            p = jnp.exp2(s - m_new)
            l_ref[:, cols] = alpha * l_ref[:, cols] + jnp.sum(p, axis=0, keepdims=True)
            acc_ref[:, cols] = alpha * acc_ref[:, cols] + _dot(vt_c, p.astype(BF16))
            m_ref[:, cols] = m_new

    def body(j, carry):
        block(j, False)
        return carry

    for a in range(ahead):
        s_ref[a] = logits(0, a)
    lax.fori_loop(0, qi, body, 0)
    block(qi, True)

    o = acc_ref[...] * (1.0 / l_ref[...])
    out = o[:, 0:tq] - lam_ref[0] * o[:, tq:r]
    ms = jnp.mean(out * out, axis=0, keepdims=True)
    out = out * lax.rsqrt(ms + SUBLN_EPS) * g_ref[...] * out_scale
    o_ref[...] = out.T.astype(BF16)


def _diff_attention(lam, qt, k, vt, subln_g_cols, lambda_init):
    b, s, d = k.shape
    tq = vt.shape[-1]
    n_blk = s // tq
    return pl.pallas_call(
        functools.partial(_attn_kernel, out_scale=1.0 - lambda_init),
        grid=(b, DIFF_HEADS, n_blk),
        in_specs=[
            pl.BlockSpec(memory_space=pltpu.SMEM),
            pl.BlockSpec((None, LANES, tq), lambda bi, h, i: (bi, h, i)),
            pl.BlockSpec((None, s, LANES), lambda bi, h, i: (bi, 0, h)),
            pl.BlockSpec((None, None, n_blk, DIFF_V_DIM, tq), lambda bi, h, i: (bi, h, 0, 0, 0)),
            _const_spec((DIFF_V_DIM, tq)),
        ],
        out_specs=pl.BlockSpec((None, tq, LANES), lambda bi, h, i: (bi, i, h)),
        out_shape=jax.ShapeDtypeStruct((b, s, d), BF16),
        scratch_shapes=[
            pltpu.VMEM((LANES, 2 * tq), BF16),
            pltpu.VMEM((1, 2 * tq), F32),
            pltpu.VMEM((1, 2 * tq), F32),
            pltpu.VMEM((DIFF_V_DIM, 2 * tq), F32),
            pltpu.VMEM((LOGITS_AHEAD, tq, MXU_COLS), F32),
        ],
        compiler_params=_params(("parallel", "parallel", "arbitrary")),
        name="diff_attention",
    )(lam, qt, k, vt, subln_g_cols)


def _split_bf16(x):
    hi = x.astype(BF16)
    lo = (x - hi.astype(F32)).astype(BF16)
    return hi, lo


def _oproj_router_kernel(a_ref, x_ref, wo_ref, g_ref, wr_hi_ref, wr_lo_ref,
                         x_out_ref, h_ref, route_ref):
    x = x_ref[...] + _dot(a_ref[...], wo_ref[...])
    x_out_ref[...] = x
    h = _rms(x, NORM_EPS) * g_ref[...]
    h_hi, h_lo = _split_bf16(h)
    h_ref[...] = h_hi
    w_hi = wr_hi_ref[...]
    logits = _dot(h_hi, w_hi) + (_dot(h_lo, w_hi) + _dot(h_hi, wr_lo_ref[...]))
    lane = lax.broadcasted_iota(jnp.int32, logits.shape, 1)
    lane_f = lane.astype(F32)
    neg = -jnp.inf
    lg = jnp.where(lane < N_EXPERTS, logits, neg)
    m1 = jnp.max(lg, axis=1, keepdims=True)
    i1 = jnp.min(jnp.where(lg == m1, lane_f, float(ROUTE_LANES)), axis=1, keepdims=True)
    lg2 = jnp.where(lane_f == i1, neg, lg)
    m2 = jnp.max(lg2, axis=1, keepdims=True)
    i2 = jnp.min(jnp.where(lg2 == m2, lane_f, float(ROUTE_LANES)), axis=1, keepdims=True)
    e = jnp.exp(m2 - m1)
    w1 = 1.0 / (1.0 + e)
    w2 = e / (1.0 + e)
    route = jnp.where(lane == 0, i1, jnp.where(lane == 1, i2,
                      jnp.where(lane == 2, w1, jnp.where(lane == 3, w2, 0.0))))
    route_ref[...] = route


def _oproj_router(attn2d, x2d, wo, g, wr_hi, wr_lo):
    t, d = x2d.shape
    tm = min(TOKEN_TILE, t)
    tok = lambda i: (i, 0)
    return pl.pallas_call(
        _oproj_router_kernel,
        grid=(t // tm,),
        in_specs=[
            pl.BlockSpec((tm, d), tok),
            pl.BlockSpec((tm, d), tok),
            _const_spec((d, d)),
            _const_spec((1, d)),
            _const_spec((d, ROUTE_LANES)),
            _const_spec((d, ROUTE_LANES)),
        ],
        out_specs=[
            pl.BlockSpec((tm, d), tok),
            pl.BlockSpec((tm, d), tok),
            pl.BlockSpec((tm, ROUTE_LANES), tok),
        ],
        out_shape=[
            jax.ShapeDtypeStruct((t, d), F32),
            jax.ShapeDtypeStruct((t, d), BF16),
            jax.ShapeDtypeStruct((t, ROUTE_LANES), F32),
        ],
        compiler_params=_params(("parallel",)),
        name="oproj_router",
    )(attn2d, x2d, wo, g, wr_hi, wr_lo)


WEIGHT_SLOTS = 2


def _expert_kernel(te_ref, nt_ref, xs_ref, rw_ref, wg_hbm, wu_hbm, wd_hbm, y_ref,
                   wg_ref, wu_ref, wd_ref, sg_ref, su_ref, sd_ref, sem, *, fc):
    i = pl.program_id(0)
    e = te_ref[i]
    active = i < nt_ref[0]
    first = jnp.logical_and(active, jnp.logical_or(i == 0, e != te_ref[jnp.maximum(i - 1, 0)]))
    n_chunks = wg_ref.shape[1] // fc

    def chunk_copies(c):
        slot = c % WEIGHT_SLOTS
        cols = pl.ds(c * fc, fc)
        return (
            pltpu.make_async_copy(wg_hbm.at[e, :, cols], sg_ref.at[slot], sem.at[0, slot]),
            pltpu.make_async_copy(wu_hbm.at[e, :, cols], su_ref.at[slot], sem.at[1, slot]),
            pltpu.make_async_copy(wd_hbm.at[e, cols, :], sd_ref.at[slot], sem.at[2, slot]),
        )

    def swiglu_chunk(xs, c):
        lo, hi = c * fc, (c + 1) * fc
        a = _dot(xs, wg_ref[:, lo:hi])
        u = _dot(xs, wu_ref[:, lo:hi])
        return _dot((_silu(a) * u).astype(BF16), wd_ref[lo:hi, :])

    def finish(acc):
        y_ref[...] = (acc * rw_ref[...]).astype(y_ref.dtype)

    @pl.when(first)
    def _():
        for c in range(min(WEIGHT_SLOTS, n_chunks)):
            for cp in chunk_copies(c):
                cp.start()
        xs = xs_ref[...]
        acc = jnp.zeros(y_ref.shape, F32)
        for c in range(n_chunks):
            slot = c % WEIGHT_SLOTS
            lo, hi = c * fc, (c + 1) * fc
            for cp in chunk_copies(c):
                cp.wait()
            wg_ref[:, lo:hi] = sg_ref[slot].astype(BF16)
            wu_ref[:, lo:hi] = su_ref[slot].astype(BF16)
            wd_ref[lo:hi, :] = sd_ref[slot].astype(BF16)
            if c + WEIGHT_SLOTS < n_chunks:
                for cp in chunk_copies(c + WEIGHT_SLOTS):
                    cp.start()
            acc = acc + swiglu_chunk(xs, c)
        finish(acc)

    @pl.when(jnp.logical_and(active, jnp.logical_not(first)))
    def _():
        xs = xs_ref[...]
        acc = jnp.zeros(y_ref.shape, F32)
        for c in range(n_chunks):
            acc = acc + swiglu_chunk(xs, c)
        finish(acc)

    @pl.when(jnp.logical_not(active))
    def _():
        y_ref[...] = jnp.zeros_like(y_ref)


def _experts(tile_expert, n_tiles_used, xs, row_w, wg, wu, wd):
    p, d = xs.shape
    f = wg.shape[2]
    tm = EXPERT_TILE
    fc = EXPERT_F_CHUNK
    assert f % fc == 0
    n_tiles = p // tm

    def row(i, te, nt):
        return (jnp.minimum(i, nt[0] - 1), 0)

    grid_spec = pltpu.PrefetchScalarGridSpec(
        num_scalar_prefetch=2,
        grid=(n_tiles,),
        in_specs=[
            pl.BlockSpec((tm, d), row),
            pl.BlockSpec((tm, 1), row),
            pl.BlockSpec(memory_space=pl.ANY),
            pl.BlockSpec(memory_space=pl.ANY),
            pl.BlockSpec(memory_space=pl.ANY),
        ],
        out_specs=pl.BlockSpec((tm, d), lambda i, te, nt: (i, 0)),
        scratch_shapes=[
            pltpu.VMEM((d, f), BF16),
            pltpu.VMEM((d, f), BF16),
            pltpu.VMEM((f, d), BF16),
            pltpu.VMEM((WEIGHT_SLOTS, d, fc), F32),
            pltpu.VMEM((WEIGHT_SLOTS, d, fc), F32),
            pltpu.VMEM((WEIGHT_SLOTS, fc, d), F32),
            pltpu.SemaphoreType.DMA((3, WEIGHT_SLOTS)),
        ],
    )
    return pl.pallas_call(
        functools.partial(_expert_kernel, fc=fc),
        grid_spec=grid_spec,
        out_shape=jax.ShapeDtypeStruct((p, d), BF16),
        compiler_params=_params(("arbitrary",)),
        name="experts",
    )(tile_expert, n_tiles_used, xs, row_w, wg, wu, wd)


def _route_tables(route, tm):
    t = route.shape[0]
    n_pairs = t * TOP_K
    n_tiles = n_pairs // tm + N_EXPERTS - 1
    p = n_tiles * tm
    e_pair = route[:, 0:TOP_K].astype(jnp.int32).reshape(n_pairs)
    w_pair = route[:, TOP_K:2 * TOP_K].reshape(n_pairs)
    onehot = (e_pair[:, None] == jnp.arange(N_EXPERTS, dtype=jnp.int32)[None, :]).astype(jnp.int32)
    counts = jnp.sum(onehot, axis=0)
    rank = jnp.sum((jnp.cumsum(onehot, axis=0) - onehot) * onehot, axis=1)
    tiles_per = (counts + tm - 1) // tm
    tile_end = jnp.cumsum(tiles_per)
    tile_start = tile_end - tiles_per
    dest = tile_start[e_pair] * tm + rank
    row_pair = jnp.full((p,), n_pairs, jnp.int32).at[dest].set(
        jnp.arange(n_pairs, dtype=jnp.int32), unique_indices=True)
    row_token = jnp.where(row_pair < n_pairs, row_pair // TOP_K, 0)
    row_w = jnp.concatenate([w_pair, jnp.zeros((1,), F32)])[row_pair]
    tile_ids = jnp.arange(n_tiles, dtype=jnp.int32)
    tile_expert = jnp.minimum(
        jnp.sum((tile_ids[:, None] >= tile_end[None, :]).astype(jnp.int32), axis=1),
        N_EXPERTS - 1)
    n_used = tile_end[-1:].astype(jnp.int32)
    return row_token, row_w.reshape(p, 1), tile_expert, n_used, dest.reshape(t, TOP_K)


def _final_kernel(x_ref, ya_ref, yb_ref, g_ref, o_ref):
    x = x_ref[...] + (ya_ref[...].astype(F32) + yb_ref[...].astype(F32))
    o_ref[...] = _rms(x, NORM_EPS) * g_ref[...]


def _final(x2d, ya, yb, g):
    t, d = x2d.shape
    tm = min(2 * TOKEN_TILE, t)
    tok = lambda i: (i, 0)
    return pl.pallas_call(
        _final_kernel,
        grid=(t // tm,),
        in_specs=[pl.BlockSpec((tm, d), tok)] * 3 + [_const_spec((1, d))],
        out_specs=pl.BlockSpec((tm, d), tok),
        out_shape=jax.ShapeDtypeStruct((t, d), F32),
        compiler_params=_params(("parallel",)),
        name="final_norm",
    )(x2d, ya, yb, g)


def _rope_angles(seq, dim):
    inv = 1.0 / (ROPE_THETA ** (jnp.arange(0, dim, 2, dtype=F32) / dim))
    return jnp.arange(seq, dtype=F32)[:, None] * inv[None, :]


def _retention_decays():
    h, c = RET_HEADS, RET_CHUNK
    log_gamma = jnp.log1p(-jnp.exp2(-5.0 - jnp.arange(h, dtype=F32)))
    pos = jnp.arange(c, dtype=F32)
    rel = pos[:, None] - pos[None, :]
    intra = jnp.where(rel >= 0, jnp.exp(log_gamma[:, None, None] * jnp.maximum(rel, 0.0)), 0.0)
    q_decay = jnp.exp(log_gamma[:, None] * (pos + 1.0))[:, :, None]
    k_decay = jnp.exp(log_gamma[:, None] * (c - 1.0 - pos))[:, :, None]
    chunk_decay = jnp.exp(log_gamma * c)
    return chunk_decay, intra, q_decay, k_decay


def kernel(x, attn_norm_g, ffn_norm_g, ret_w_in, ret_w_out, kv_norm_g, w_kv, diff_w_q,
           diff_lam_q1, diff_lam_k1, diff_lam_q2, diff_lam_k2, diff_subln_g, diff_w_o,
           ffn_w_gate, ffn_w_up, ffn_w_down, moe_w_router, moe_w_gate, moe_w_up, moe_w_down,
           final_norm_g):
    b, s, d = x.shape
    t = b * s
    row = lambda g: g.reshape(1, -1).astype(F32)
    bf = lambda w: w.astype(BF16)

    ang = _rope_angles(s, RET_QK_DIM)
    q, k, v, gate = _ret_proj(x.reshape(t, d), row(attn_norm_g[0]), bf(ret_w_in[0]),
                              jnp.cos(ang), jnp.sin(ang), s)
    chunk_decay, intra_decay, q_decay, k_decay = _retention_decays()
    x1 = _retention(chunk_decay, q.reshape(b, s, d), k.reshape(b, s, d),
                    v.reshape(b, s, 2 * d), gate.reshape(b, s, 2 * d), x,
                    bf(ret_w_out[0]), intra_decay, q_decay, k_decay)
    x2 = _ffn(x1.reshape(t, d), row(ffn_norm_g[0]), bf(ffn_w_gate[0]), bf(ffn_w_up[0]),
              bf(ffn_w_down[0]))

    ang = _rope_angles(s, DIFF_HEAD_DIM)
    cos32, sin32 = lax.optimization_barrier((jnp.cos(ang), jnp.sin(ang)))
    zero32 = jnp.zeros_like(sin32)
    pairs = LANES // DIFF_HEAD_DIM
    cos = jnp.concatenate([cos32, cos32] * pairs, axis=-1)
    sin_a = jnp.concatenate([-sin32, zero32] * pairs, axis=-1)
    sin_b = jnp.concatenate([zero32, sin32] * pairs, axis=-1)
    ks, vt, qt = _kvq_proj(x2.reshape(b, s, d), row(kv_norm_g), row(attn_norm_g[1]),
                           bf(w_kv[:, :d]), bf(w_kv[:, d:].T), bf(diff_w_q[0].T),
                           cos, sin_a, sin_b, cos32.T, sin32.T)
    lambda_init = 0.8 - 0.6 * math.exp(-0.3 * 1)
    lam = (jnp.exp(jnp.sum(diff_lam_q1[0].astype(F32) * diff_lam_k1[0].astype(F32)))
           - jnp.exp(jnp.sum(diff_lam_q2[0].astype(F32) * diff_lam_k2[0].astype(F32)))
           + lambda_init).reshape(1)
    g_cols = jnp.broadcast_to(diff_subln_g[0].astype(F32)[:, None], (DIFF_V_DIM, vt.shape[-1]))
    attn = _diff_attention(lam, qt, ks, vt, g_cols, lambda_init)

    wr = jnp.zeros((d, ROUTE_LANES), F32).at[:, :N_EXPERTS].set(moe_w_router[0])
    wr_hi = wr.astype(BF16)
    wr_lo = (wr - wr_hi.astype(F32)).astype(BF16)
    x3, h, route = _oproj_router(attn.reshape(t, d), x2, bf(diff_w_o[0]), row(ffn_norm_g[1]),
                                 wr_hi, wr_lo)
    row_token, row_w, tile_expert, n_used, dest = _route_tables(route, EXPERT_TILE)
    xs = jnp.take(h, row_token, axis=0, mode="clip")
    y = _experts(tile_expert, n_used, xs, row_w, moe_w_gate[0], moe_w_up[0], moe_w_down[0])
    ya = jnp.take(y, dest[:, 0], axis=0, mode="clip")
    yb = jnp.take(y, dest[:, 1], axis=0, mode="clip")
    out = _final(x3, ya, yb, row(final_norm_g))
    return out.reshape(b, s, d)
```

```python
import functools
import math

import jax
import jax.numpy as jnp
from jax import lax
from jax.experimental import pallas as pl
from jax.experimental.pallas import tpu as pltpu

F32 = jnp.float32
BF16 = jnp.bfloat16

D_MODEL = 1024
RET_HEADS = 4
RET_QK_DIM = D_MODEL // RET_HEADS
RET_V_DIM = 2 * D_MODEL // RET_HEADS
RET_CHUNK = 256
DIFF_HEADS = 8
DIFF_HEAD_DIM = D_MODEL // DIFF_HEADS // 2
DIFF_V_DIM = 2 * DIFF_HEAD_DIM
N_EXPERTS = 8
TOP_K = 2
ROPE_THETA = 10000.0
NORM_EPS = 1e-6
SUBLN_EPS = 1e-5
LOG2E = math.log2(math.e)

LANES = 128
VMEM_LIMIT = 56 * 1024 * 1024

TOKEN_TILE = 512
ATTN_TILE = 2048
EXPERT_TILE = 512
EXPERT_F_CHUNK = 512


def _dot(a, b):
    return jnp.dot(a, b, preferred_element_type=F32)


def _dot_nt(a, b):
    return lax.dot_general(a, b, (((1,), (1,)), ((), ())), preferred_element_type=F32)


def _dot_tn(a, b):
    return lax.dot_general(a, b, (((0,), (0,)), ((), ())), preferred_element_type=F32)


def _rms(x, eps):
    return x * lax.rsqrt(jnp.mean(x * x, axis=-1, keepdims=True) + eps)


def _silu(x):
    return x * (1.0 / (1.0 + jnp.exp(-x)))


def _const_spec(shape):
    zeros = (0,) * len(shape)
    return pl.BlockSpec(shape, lambda *_: zeros, pipeline_mode=pl.Buffered(1))


def _params(semantics):
    return pltpu.CompilerParams(dimension_semantics=semantics, vmem_limit_bytes=VMEM_LIMIT)


def _ret_proj_kernel(x_ref, g_ref, w_ref, cos_ref, sin_ref, q_ref, k_ref, v_ref, gate_ref):
    d = D_MODEL
    hb = (_rms(x_ref[...], NORM_EPS) * g_ref[...]).astype(BF16)
    c = cos_ref[...]
    s = sin_ref[...]
    half = RET_QK_DIM // 2
    for dst, col0, scale in ((q_ref, 0, 1.0), (k_ref, d, RET_QK_DIM ** -0.5)):
        y = _dot(hb, w_ref[:, col0:col0 + d])
        for h in range(RET_HEADS):
            lo = h * RET_QK_DIM
            y1 = y[:, lo:lo + half]
            y2 = y[:, lo + half:lo + 2 * half]
            dst[:, lo:lo + half] = ((y1 * c - y2 * s) * scale).astype(BF16)
            dst[:, lo + half:lo + 2 * half] = ((y2 * c + y1 * s) * scale).astype(BF16)
    for j in range(2):
        v_ref[:, j * d:(j + 1) * d] = _dot(hb, w_ref[:, (2 + j) * d:(3 + j) * d]).astype(BF16)
        gate_ref[:, j * d:(j + 1) * d] = _silu(
            _dot(hb, w_ref[:, (4 + j) * d:(5 + j) * d])).astype(BF16)


def _ret_proj(x2d, g, w_in, cos, sin, seq):
    t, d = x2d.shape
    tm = min(TOKEN_TILE, seq)
    n_seq = seq // tm
    tok = lambda i: (i, 0)
    pos = lambda i: (i % n_seq, 0)
    return pl.pallas_call(
        _ret_proj_kernel,
        grid=(t // tm,),
        in_specs=[
            pl.BlockSpec((tm, d), tok),
            _const_spec((1, d)),
            _const_spec((d, 6 * d)),
            pl.BlockSpec((tm, LANES), pos),
            pl.BlockSpec((tm, LANES), pos),
        ],
        out_specs=[
            pl.BlockSpec((tm, d), tok),
            pl.BlockSpec((tm, d), tok),
            pl.BlockSpec((tm, 2 * d), tok),
            pl.BlockSpec((tm, 2 * d), tok),
        ],
        out_shape=[
            jax.ShapeDtypeStruct((t, d), BF16),
            jax.ShapeDtypeStruct((t, d), BF16),
            jax.ShapeDtypeStruct((t, 2 * d), BF16),
            jax.ShapeDtypeStruct((t, 2 * d), BF16),
        ],
        compiler_params=_params(("parallel",)),
        name="ret_proj",
    )(x2d, g, w_in, cos, sin)


def _retention_kernel(cd_ref, q_ref, k_ref, v_ref, gate_ref, x_ref, wout_ref,
                      intra_ref, qdec_ref, kdec_ref, o_ref, state_ref, gated_ref):
    c = RET_CHUNK
    n_chunks = q_ref.shape[0] // c

    @pl.when(pl.program_id(1) == 0)
    def _():
        state_ref[...] = jnp.zeros_like(state_ref)

    for h in range(RET_HEADS):
        qk = slice(h * RET_QK_DIM, (h + 1) * RET_QK_DIM)
        vv = slice(h * RET_V_DIM, (h + 1) * RET_V_DIM)
        for n in range(n_chunks):
            rows = slice(n * c, (n + 1) * c)
            qi = q_ref[rows, qk]
            ki = k_ref[rows, qk]
            vi = v_ref[rows, vv]
            scores = _dot_nt(qi, ki) * intra_ref[h]
            intra = _dot(scores.astype(BF16), vi)
            state = state_ref[h]
            inter = _dot(qi, state.astype(BF16)) * qdec_ref[h]
            kd = (ki.astype(F32) * kdec_ref[h]).astype(BF16)
            state_ref[h] = state * cd_ref[h] + _dot_tn(kd, vi)
            o = _rms(intra + inter, NORM_EPS)
            gated_ref[rows, vv] = (gate_ref[rows, vv].astype(F32) * o).astype(BF16)
    o_ref[...] = x_ref[...] + _dot(gated_ref[...], wout_ref[...])


def _retention(chunk_decay, q, k, v, gate, x, w_out, intra_decay, q_decay, k_decay):
    b, s, d = x.shape
    ts = min(TOKEN_TILE, s)
    blk = lambda w: pl.BlockSpec((None, ts, w), lambda bi, si: (bi, si, 0))
    return pl.pallas_call(
        _retention_kernel,
        grid=(b, s // ts),
        in_specs=[
            pl.BlockSpec(memory_space=pltpu.SMEM),
            blk(d), blk(d), blk(2 * d), blk(2 * d), blk(d),
            _const_spec((2 * d, d)),
            _const_spec((RET_HEADS, RET_CHUNK, RET_CHUNK)),
            _const_spec((RET_HEADS, RET_CHUNK, 1)),
            _const_spec((RET_HEADS, RET_CHUNK, 1)),
        ],
        out_specs=blk(d),
        out_shape=jax.ShapeDtypeStruct((b, s, d), F32),
        scratch_shapes=[
            pltpu.VMEM((RET_HEADS, RET_QK_DIM, RET_V_DIM), F32),
            pltpu.VMEM((ts, 2 * d), BF16),
        ],
        compiler_params=_params(("parallel", "arbitrary")),
        name="retention",
    )(chunk_decay, q, k, v, gate, x, w_out, intra_decay, q_decay, k_decay)


def _ffn_kernel(x_ref, g_ref, wg_ref, wu_ref, wd_ref, o_ref, *, f_chunks):
    x = x_ref[...]
    hb = (_rms(x, NORM_EPS) * g_ref[...]).astype(BF16)
    acc = x
    for lo, hi in f_chunks:
        a = _dot(hb, wg_ref[:, lo:hi])
        u = _dot(hb, wu_ref[:, lo:hi])
        acc = acc + _dot((_silu(a) * u).astype(BF16), wd_ref[lo:hi, :])
    o_ref[...] = acc


def _chunks(total, size):
    return tuple((lo, min(lo + size, total)) for lo in range(0, total, size))


def _ffn(x2d, g, wg, wu, wd):
    t, d = x2d.shape
    f = wg.shape[1]
    tm = min(TOKEN_TILE, t)
    tok = lambda i: (i, 0)
    return pl.pallas_call(
        functools.partial(_ffn_kernel, f_chunks=_chunks(f, 1024)),
        grid=(t // tm,),
        in_specs=[
            pl.BlockSpec((tm, d), tok),
            _const_spec((1, d)),
            _const_spec((d, f)),
            _const_spec((d, f)),
            _const_spec((f, d)),
        ],
        out_specs=pl.BlockSpec((tm, d), tok),
        out_shape=jax.ShapeDtypeStruct((t, d), F32),
        compiler_params=_params(("parallel",)),
        name="dense_ffn",
    )(x2d, g, wg, wu, wd)


def _rope64(y, cos, sin_a, sin_b):
    half = DIFF_HEAD_DIM // 2
    return (y * cos + pltpu.roll(y, LANES - half, 1) * sin_a
            + pltpu.roll(y, half, 1) * sin_b)


def _kvq_proj_kernel(x_ref, gkv_ref, gq_ref, wk_ref, wvt_ref, wqt_ref, cos_ref, sa_ref, sb_ref,
                     cost_ref, sint_ref, k_ref, vt_ref, qt_ref):
    d = D_MODEL
    xn = _rms(x_ref[...], NORM_EPS)
    hkv = (xn * gkv_ref[...]).astype(BF16)
    hq = (xn * gq_ref[...]).astype(BF16)
    cos = cos_ref[...]
    sa = sa_ref[...]
    sb = sb_ref[...]
    yk = _dot(hkv, wk_ref[...])
    for j in range(d // LANES):
        cols = slice(j * LANES, (j + 1) * LANES)
        k_ref[:, cols] = _rope64(yk[:, cols], cos, sa, sb).astype(BF16)

    yvt = _dot_nt(wvt_ref[...], hkv)
    for h in range(DIFF_HEADS):
        vt_ref[h] = yvt[h * DIFF_V_DIM:(h + 1) * DIFF_V_DIM, :].astype(BF16)

    yqt = _dot_nt(wqt_ref[...], hq)
    ct = cost_ref[...]
    st = sint_ref[...]
    half = DIFF_HEAD_DIM // 2
    q_scale = DIFF_HEAD_DIM ** -0.5 * LOG2E
    for g in range(d // DIFF_HEAD_DIM):
        r0 = g * DIFF_HEAD_DIM
        a = yqt[r0:r0 + half, :]
        b = yqt[r0 + half:r0 + 2 * half, :]
        qt_ref[r0:r0 + half, :] = ((a * ct - b * st) * q_scale).astype(BF16)
        qt_ref[r0 + half:r0 + 2 * half, :] = ((b * ct + a * st) * q_scale).astype(BF16)


def _kvq_proj(x, gkv, gq, wk, wvt, wqt, cos, sin_a, sin_b, cos_t, sin_t):
    b, s, d = x.shape
    tm = min(TOKEN_TILE, s)
    n_seq = s // tm
    tk = min(ATTN_TILE, s)
    per_blk = tk // tm
    half = DIFF_HEAD_DIM // 2
    pos = lambda bi, i: (i, 0)
    pos_t = lambda bi, i: (0, i)
    return pl.pallas_call(
        _kvq_proj_kernel,
        grid=(b, n_seq),
        in_specs=[
            pl.BlockSpec((None, tm, d), lambda bi, i: (bi, i, 0)),
            _const_spec((1, d)),
            _const_spec((1, d)),
            _const_spec((d, d)),
            _const_spec((d, d)),
            _const_spec((d, d)),
            pl.BlockSpec((tm, LANES), pos),
            pl.BlockSpec((tm, LANES), pos),
            pl.BlockSpec((tm, LANES), pos),
            pl.BlockSpec((half, tm), pos_t),
            pl.BlockSpec((half, tm), pos_t),
        ],
        out_specs=[
            pl.BlockSpec((None, tm, d), lambda bi, i: (bi, i, 0)),
            pl.BlockSpec((None, DIFF_HEADS, None, DIFF_V_DIM, tm),
                         lambda bi, i: (bi, 0, i // per_blk, 0, i % per_blk)),
            pl.BlockSpec((None, d, tm), lambda bi, i: (bi, 0, i)),
        ],
        out_shape=[
            jax.ShapeDtypeStruct((b, s, d), BF16),
            jax.ShapeDtypeStruct((b, DIFF_HEADS, s // tk, DIFF_V_DIM, tk), BF16),
            jax.ShapeDtypeStruct((b, d, s), BF16),
        ],
        compiler_params=_params(("parallel", "parallel")),
        name="kvq_proj",
    )(x, gkv, gq, wk, wvt, wqt, cos, sin_a, sin_b, cos_t, sin_t)


MXU_COLS = 512
MASKED_LOGIT = -1e30
LOGITS_AHEAD = 1


def _attn_kernel(lam_ref, qt_ref, k_ref, vt_ref, g_ref, o_ref, qq_ref, m_ref, l_ref, acc_ref,
                 s_ref, *, out_scale):
    tq = qt_ref.shape[1]
    tk = vt_ref.shape[2]
    assert tq == tk
    r = 2 * tq
    qi = pl.program_id(2)

    qt = qt_ref[...]
    feat = lax.broadcasted_iota(jnp.int32, qt.shape, 0)
    zero = jnp.zeros_like(qt)
    qq_ref[:, 0:tq] = jnp.where(feat < DIFF_HEAD_DIM, qt, zero)
    qq_ref[:, tq:r] = jnp.where(feat >= DIFF_HEAD_DIM, qt, zero)
    m_ref[...] = jnp.full_like(m_ref, MASKED_LOGIT)
    l_ref[...] = jnp.zeros_like(l_ref)
    acc_ref[...] = jnp.zeros_like(acc_ref)

    n_groups = r // MXU_COLS
    ahead = s_ref.shape[0]
    assert n_groups >= ahead

    def keys_needed(c):
        return min(tk, (c * MXU_COLS) % tq + MXU_COLS)

    def logits(j, c, n_keys=tk):
        start = pl.multiple_of(j * tk, tk)
        return _dot(k_ref[pl.ds(start, n_keys), :],
                    qq_ref[:, c * MXU_COLS:(c + 1) * MXU_COLS])

    def block(j, is_diagonal):
        vt = vt_ref[j]
        pending = [s_ref[a] for a in range(ahead)]
        for c in range(n_groups):
            cols = slice(c * MXU_COLS, (c + 1) * MXU_COLS)
            s = pending.pop(0)
            if c + ahead < n_groups:
                n_next = keys_needed(c + ahead) if is_diagonal else tk
                pending.append(logits(j, c + ahead, n_next))
            elif not is_diagonal:
                s_ref[c + ahead - n_groups] = logits(j + 1, c + ahead - n_groups)
            vt_c = vt
            if is_diagonal:
                n_keys = keys_needed(c)
                s = s[0:n_keys, :]
                vt_c = vt[:, 0:n_keys]
                key = lax.broadcasted_iota(jnp.int32, s.shape, 0)
                qry = lax.broadcasted_iota(jnp.int32, s.shape, 1) + (c * MXU_COLS) % tq
                s = jnp.where(key <= qry, s, MASKED_LOGIT)
            m_prev = m_ref[:, cols]
            m_new = jnp.maximum(m_prev, jnp.max(s, axis=0, keepdims=True))
            alpha = jnp.exp2(m_prev - m_new)
            p = jnp.exp2(s - m_new)
            l_ref[:, cols] = alpha * l_ref[:, cols] + jnp.sum(p, axis=0, keepdims=True)
            acc_ref[:, cols] = alpha * acc_ref[:, cols] + _dot(vt_c, p.astype(BF16))
            m_ref[:, cols] = m_new

    def body(j, carry):
        block(j, False)
        return carry

    for a in range(ahead):
        s_ref[a] = logits(0, a)
    lax.fori_loop(0, qi, body, 0)
    block(qi, True)

    o = acc_ref[...] * (1.0 / l_ref[...])
    out = o[:, 0:tq] - lam_ref[0] * o[:, tq:r]
    ms = jnp.mean(out * out, axis=0, keepdims=True)
    out = out * lax.rsqrt(ms + SUBLN_EPS) * g_ref[...] * out_scale
    o_ref[...] = out.T.astype(BF16)


def _diff_attention(lam, qt, k, vt, subln_g_cols, lambda_init):
    b, s, d = k.shape
    tq = vt.shape[-1]
    n_blk = s // tq
    return pl.pallas_call(
        functools.partial(_attn_kernel, out_scale=1.0 - lambda_init),
        grid=(b, DIFF_HEADS, n_blk),
        in_specs=[
            pl.BlockSpec(memory_space=pltpu.SMEM),
            pl.BlockSpec((None, LANES, tq), lambda bi, h, i: (bi, h, i)),
            pl.BlockSpec((None, s, LANES), lambda bi, h, i: (bi, 0, h)),
            pl.BlockSpec((None, None, n_blk, DIFF_V_DIM, tq), lambda bi, h, i: (bi, h, 0, 0, 0)),
            _const_spec((DIFF_V_DIM, tq)),
        ],
        out_specs=pl.BlockSpec((None, tq, LANES), lambda bi, h, i: (bi, i, h)),
        out_shape=jax.ShapeDtypeStruct((b, s, d), BF16),
        scratch_shapes=[
            pltpu.VMEM((LANES, 2 * tq), BF16),
            pltpu.VMEM((1, 2 * tq), F32),
            pltpu.VMEM((1, 2 * tq), F32),
            pltpu.VMEM((DIFF_V_DIM, 2 * tq), F32),
            pltpu.VMEM((LOGITS_AHEAD, tq, MXU_COLS), F32),
        ],
        compiler_params=_params(("parallel", "parallel", "arbitrary")),
        name="diff_attention",
    )(lam, qt, k, vt, subln_g_cols)


def _split_bf16(x):
    hi = x.astype(BF16)
    lo = (x - hi.astype(F32)).astype(BF16)
    return hi, lo


ROUTE_ROWS = 8


def _oproj_router_kernel(a_ref, x_ref, wo_ref, g_ref, wrt_ref, x_out_ref, h_ref, route_ref):
    x = x_ref[...] + _dot(a_ref[...], wo_ref[...])
    x_out_ref[...] = x
    h = _rms(x, NORM_EPS) * g_ref[...]
    h_hi, h_lo = _split_bf16(h)
    h_ref[...] = h_hi
    wrt = wrt_ref[...]
    a = _dot_nt(wrt, h_hi)
    b = _dot_nt(wrt, h_lo)
    ne = N_EXPERTS
    logits = a[0:ne, :] + (b[0:ne, :] + a[ne:2 * ne, :])
    expert = lax.broadcasted_iota(jnp.int32, logits.shape, 0).astype(F32)
    neg = -jnp.inf
    m1 = jnp.max(logits, axis=0, keepdims=True)
    i1 = jnp.min(jnp.where(logits == m1, expert, float(ne)), axis=0, keepdims=True)
    rest = jnp.where(expert == i1, neg, logits)
    m2 = jnp.max(rest, axis=0, keepdims=True)
    i2 = jnp.min(jnp.where(rest == m2, expert, float(ne)), axis=0, keepdims=True)
    e = jnp.exp(m2 - m1)
    w1 = 1.0 / (1.0 + e)
    w2 = e / (1.0 + e)
    rec = lax.broadcasted_iota(jnp.int32, (ROUTE_ROWS, logits.shape[1]), 0)
    route_ref[...] = jnp.where(rec == 0, i1, jnp.where(rec == 1, i2,
                               jnp.where(rec == 2, w1, jnp.where(rec == 3, w2, 0.0))))


def _oproj_router(attn2d, x2d, wo, g, wrt):
    t, d = x2d.shape
    tm = min(TOKEN_TILE, t)
    tok = lambda i: (i, 0)
    return pl.pallas_call(
        _oproj_router_kernel,
        grid=(t // tm,),
        in_specs=[
            pl.BlockSpec((tm, d), tok),
            pl.BlockSpec((tm, d), tok),
            _const_spec((d, d)),
            _const_spec((1, d)),
            _const_spec((2 * N_EXPERTS, d)),
        ],
        out_specs=[
            pl.BlockSpec((tm, d), tok),
            pl.BlockSpec((tm, d), tok),
            pl.BlockSpec((ROUTE_ROWS, tm), lambda i: (0, i)),
        ],
        out_shape=[
            jax.ShapeDtypeStruct((t, d), F32),
            jax.ShapeDtypeStruct((t, d), BF16),
            jax.ShapeDtypeStruct((ROUTE_ROWS, t), F32),
        ],
        compiler_params=_params(("parallel",)),
        name="oproj_router",
    )(attn2d, x2d, wo, g, wrt)


WEIGHT_SLOTS = 2


def _expert_kernel(te_ref, nt_ref, xs_ref, rw_ref, wg_hbm, wu_hbm, wd_hbm, y_prev_hbm, y_ref,
                   wg_ref, wu_ref, wd_ref, sg_ref, su_ref, sd_ref, sem, *, fc):
    del y_prev_hbm
    i = pl.program_id(0)
    e = te_ref[i]
    active = i < nt_ref[0]
    first = jnp.logical_and(active, jnp.logical_or(i == 0, e != te_ref[jnp.maximum(i - 1, 0)]))
    n_chunks = wg_ref.shape[1] // fc

    def chunk_copies(c):
        slot = c % WEIGHT_SLOTS
        cols = pl.ds(c * fc, fc)
        return (
            pltpu.make_async_copy(wg_hbm.at[e, :, cols], sg_ref.at[slot], sem.at[0, slot]),
            pltpu.make_async_copy(wu_hbm.at[e, :, cols], su_ref.at[slot], sem.at[1, slot]),
            pltpu.make_async_copy(wd_hbm.at[e, cols, :], sd_ref.at[slot], sem.at[2, slot]),
        )

    def swiglu_chunk(xs, c):
        lo, hi = c * fc, (c + 1) * fc
        a = _dot(xs, wg_ref[:, lo:hi])
        u = _dot(xs, wu_ref[:, lo:hi])
        return _dot((_silu(a) * u).astype(BF16), wd_ref[lo:hi, :])

    def finish(acc):
        y_ref[...] = (acc * rw_ref[...]).astype(y_ref.dtype)

    @pl.when(first)
    def _():
        for c in range(min(WEIGHT_SLOTS, n_chunks)):
            for cp in chunk_copies(c):
                cp.start()
        xs = xs_ref[...]
        acc = jnp.zeros(y_ref.shape, F32)
        for c in range(n_chunks):
            slot = c % WEIGHT_SLOTS
            lo, hi = c * fc, (c + 1) * fc
            for cp in chunk_copies(c):
                cp.wait()
            wg_ref[:, lo:hi] = sg_ref[slot].astype(BF16)
            wu_ref[:, lo:hi] = su_ref[slot].astype(BF16)
            wd_ref[lo:hi, :] = sd_ref[slot].astype(BF16)
            if c + WEIGHT_SLOTS < n_chunks:
                for cp in chunk_copies(c + WEIGHT_SLOTS):
                    cp.start()
            acc = acc + swiglu_chunk(xs, c)
        finish(acc)

    @pl.when(jnp.logical_and(active, jnp.logical_not(first)))
    def _():
        xs = xs_ref[...]
        acc = jnp.zeros(y_ref.shape, F32)
        for c in range(n_chunks):
            acc = acc + swiglu_chunk(xs, c)
        finish(acc)

    @pl.when(jnp.logical_not(active))
    def _():
        y_ref[...] = jnp.zeros_like(y_ref)


def _experts(tile_expert, n_tiles_used, xs, row_w, wg, wu, wd, y_prev, *, tile_lo):
    p, d = xs.shape
    f = wg.shape[2]
    tm = EXPERT_TILE
    fc = EXPERT_F_CHUNK
    assert f % fc == 0
    n_tiles = p // tm

    def row(i, te, nt):
        return (jnp.maximum(jnp.minimum(i, nt[0] - 1), 0), 0)

    operands = (tile_expert, n_tiles_used, xs, row_w, wg, wu, wd, y_prev)
    grid_spec = pltpu.PrefetchScalarGridSpec(
        num_scalar_prefetch=2,
        grid=(n_tiles,),
        in_specs=[
            pl.BlockSpec((tm, d), row),
            pl.BlockSpec((tm, 1), row),
            pl.BlockSpec(memory_space=pl.ANY),
            pl.BlockSpec(memory_space=pl.ANY),
            pl.BlockSpec(memory_space=pl.ANY),
            pl.BlockSpec(memory_space=pl.ANY),
        ],
        out_specs=pl.BlockSpec((tm, d), lambda i, te, nt: (i + tile_lo, 0)),
        scratch_shapes=[
            pltpu.VMEM((d, f), BF16),
            pltpu.VMEM((d, f), BF16),
            pltpu.VMEM((f, d), BF16),
            pltpu.VMEM((WEIGHT_SLOTS, d, fc), F32),
            pltpu.VMEM((WEIGHT_SLOTS, d, fc), F32),
            pltpu.VMEM((WEIGHT_SLOTS, fc, d), F32),
            pltpu.SemaphoreType.DMA((3, WEIGHT_SLOTS)),
        ],
    )
    return pl.pallas_call(
        functools.partial(_expert_kernel, fc=fc),
        grid_spec=grid_spec,
        out_shape=jax.ShapeDtypeStruct(y_prev.shape, y_prev.dtype),
        input_output_aliases={len(operands) - 1: 0},
        compiler_params=_params(("arbitrary",)),
        name="experts",
    )(*operands)


def _route_tables(route, tm):
    t = route.shape[1]
    n_pairs = t * TOP_K
    n_tiles = n_pairs // tm + N_EXPERTS - 1
    p = n_tiles * tm
    e_pair = route[0:TOP_K, :].T.astype(jnp.int32).reshape(n_pairs)
    w_pair = route[TOP_K:2 * TOP_K, :].T.reshape(n_pairs)
    onehot = (e_pair[:, None] == jnp.arange(N_EXPERTS, dtype=jnp.int32)[None, :]).astype(jnp.int32)
    counts = jnp.sum(onehot, axis=0)
    rank = jnp.sum((jnp.cumsum(onehot, axis=0) - onehot) * onehot, axis=1)
    tiles_per = (counts + tm - 1) // tm
    tile_end = jnp.cumsum(tiles_per)
    tile_start = tile_end - tiles_per
    dest = tile_start[e_pair] * tm + rank
    row_pair = jnp.full((p,), n_pairs, jnp.int32).at[dest].set(
        jnp.arange(n_pairs, dtype=jnp.int32), unique_indices=True)
    row_token = jnp.where(row_pair < n_pairs, row_pair // TOP_K, 0)
    row_w = jnp.concatenate([w_pair, jnp.zeros((1,), F32)])[row_pair]
    tile_ids = jnp.arange(n_tiles, dtype=jnp.int32)
    tile_expert = jnp.minimum(
        jnp.sum((tile_ids[:, None] >= tile_end[None, :]).astype(jnp.int32), axis=1),
        N_EXPERTS - 1)
    n_used = tile_end[-1:].astype(jnp.int32)
    return row_token, row_w.reshape(p, 1), tile_expert, n_used, dest.reshape(t, TOP_K)


def _final_kernel(x_ref, ya_ref, yb_ref, g_ref, o_ref):
    x = x_ref[...] + (ya_ref[...].astype(F32) + yb_ref[...].astype(F32))
    o_ref[...] = _rms(x, NORM_EPS) * g_ref[...]


def _final(x2d, ya, yb, g):
    t, d = x2d.shape
    tm = min(2 * TOKEN_TILE, t)
    tok = lambda i: (i, 0)
    return pl.pallas_call(
        _final_kernel,
        grid=(t // tm,),
        in_specs=[pl.BlockSpec((tm, d), tok)] * 3 + [_const_spec((1, d))],
        out_specs=pl.BlockSpec((tm, d), tok),
        out_shape=jax.ShapeDtypeStruct((t, d), F32),
        compiler_params=_params(("parallel",)),
        name="final_norm",
    )(x2d, ya, yb, g)


def _rope_angles(seq, dim):
    inv = 1.0 / (ROPE_THETA ** (jnp.arange(0, dim, 2, dtype=F32) / dim))
    return jnp.arange(seq, dtype=F32)[:, None] * inv[None, :]


def _retention_decays():
    h, c = RET_HEADS, RET_CHUNK
    log_gamma = jnp.log1p(-jnp.exp2(-5.0 - jnp.arange(h, dtype=F32)))
    pos = jnp.arange(c, dtype=F32)
    rel = pos[:, None] - pos[None, :]
    intra = jnp.where(rel >= 0, jnp.exp(log_gamma[:, None, None] * jnp.maximum(rel, 0.0)), 0.0)
    q_decay = jnp.exp(log_gamma[:, None] * (pos + 1.0))[:, :, None]
    k_decay = jnp.exp(log_gamma[:, None] * (c - 1.0 - pos))[:, :, None]
    chunk_decay = jnp.exp(log_gamma * c)
    return chunk_decay, intra, q_decay, k_decay


def kernel(x, attn_norm_g, ffn_norm_g, ret_w_in, ret_w_out, kv_norm_g, w_kv, diff_w_q,
           diff_lam_q1, diff_lam_k1, diff_lam_q2, diff_lam_k2, diff_subln_g, diff_w_o,
           ffn_w_gate, ffn_w_up, ffn_w_down, moe_w_router, moe_w_gate, moe_w_up, moe_w_down,
           final_norm_g):
    b, s, d = x.shape
    t = b * s
    row = lambda g: g.reshape(1, -1).astype(F32)
    bf = lambda w: w.astype(BF16)

    ang = _rope_angles(s, RET_QK_DIM)
    q, k, v, gate = _ret_proj(x.reshape(t, d), row(attn_norm_g[0]), bf(ret_w_in[0]),
                              jnp.cos(ang), jnp.sin(ang), s)
    chunk_decay, intra_decay, q_decay, k_decay = _retention_decays()
    x1 = _retention(chunk_decay, q.reshape(b, s, d), k.reshape(b, s, d),
                    v.reshape(b, s, 2 * d), gate.reshape(b, s, 2 * d), x,
                    bf(ret_w_out[0]), intra_decay, q_decay, k_decay)
    x2 = _ffn(x1.reshape(t, d), row(ffn_norm_g[0]), bf(ffn_w_gate[0]), bf(ffn_w_up[0]),
              bf(ffn_w_down[0]))

    ang = _rope_angles(s, DIFF_HEAD_DIM)
    cos32, sin32 = lax.optimization_barrier((jnp.cos(ang), jnp.sin(ang)))
    zero32 = jnp.zeros_like(sin32)
    pairs = LANES // DIFF_HEAD_DIM
    cos = jnp.concatenate([cos32, cos32] * pairs, axis=-1)
    sin_a = jnp.concatenate([-sin32, zero32] * pairs, axis=-1)
    sin_b = jnp.concatenate([zero32, sin32] * pairs, axis=-1)
    ks, vt, qt = _kvq_proj(x2.reshape(b, s, d), row(kv_norm_g), row(attn_norm_g[1]),
                           bf(w_kv[:, :d]), bf(w_kv[:, d:].T), bf(diff_w_q[0].T),
                           cos, sin_a, sin_b, cos32.T, sin32.T)
    lambda_init = 0.8 - 0.6 * math.exp(-0.3 * 1)
    lam = (jnp.exp(jnp.sum(diff_lam_q1[0].astype(F32) * diff_lam_k1[0].astype(F32)))
           - jnp.exp(jnp.sum(diff_lam_q2[0].astype(F32) * diff_lam_k2[0].astype(F32)))
           + lambda_init).reshape(1)
    g_cols = jnp.broadcast_to(diff_subln_g[0].astype(F32)[:, None], (DIFF_V_DIM, vt.shape[-1]))
    attn = _diff_attention(lam, qt, ks, vt, g_cols, lambda_init)

    wr_t = moe_w_router[0].astype(F32).T
    wr_hi = wr_t.astype(BF16)
    wr_lo = (wr_t - wr_hi.astype(F32)).astype(BF16)
    x3, h, route = _oproj_router(attn.reshape(t, d), x2, bf(diff_w_o[0]), row(ffn_norm_g[1]),
                                 jnp.concatenate([wr_hi, wr_lo], axis=0))
    row_token, row_w, tile_expert, n_used, dest = _route_tables(route, EXPERT_TILE)
    tm = EXPERT_TILE
    n_tiles = tile_expert.shape[0]
    y = jnp.zeros((n_tiles * tm, d), BF16)
    for lo, hi in ((0, n_tiles // 2), (n_tiles // 2, n_tiles)):
        xs = jnp.take(h, row_token[lo * tm:hi * tm], axis=0, mode="clip")
        y = _experts(tile_expert[lo:hi], jnp.clip(n_used - lo, 0, hi - lo), xs,
                     row_w[lo * tm:hi * tm], moe_w_gate[0], moe_w_up[0], moe_w_down[0], y,
                     tile_lo=lo)
    ya = jnp.take(y, dest[:, 0], axis=0, mode="clip")
    yb = jnp.take(y, dest[:, 1], axis=0, mode="clip")
    out = _final(x3, ya, yb, row(final_norm_g))
    return out.reshape(b, s, d)
```

```python
import functools
import math

import jax
import jax.numpy as jnp
from jax import lax
from jax.experimental import pallas as pl
from jax.experimental.pallas import tpu as pltpu

F32 = jnp.float32
BF16 = jnp.bfloat16

D_MODEL = 1024
RET_HEADS = 4
RET_QK_DIM = D_MODEL // RET_HEADS
RET_V_DIM = 2 * D_MODEL // RET_HEADS
RET_CHUNK = 256
DIFF_HEADS = 8
DIFF_HEAD_DIM = D_MODEL // DIFF_HEADS // 2
DIFF_V_DIM = 2 * DIFF_HEAD_DIM
N_EXPERTS = 8
TOP_K = 2
ROPE_THETA = 10000.0
NORM_EPS = 1e-6
SUBLN_EPS = 1e-5
LOG2E = math.log2(math.e)

LANES = 128
VMEM_LIMIT = 56 * 1024 * 1024

TOKEN_TILE = 512
ATTN_TILE = 2048
EXPERT_TILE = 512
EXPERT_F_CHUNK = 512


def _dot(a, b):
    return jnp.dot(a, b, preferred_element_type=F32)


def _dot_nt(a, b):
    return lax.dot_general(a, b, (((1,), (1,)), ((), ())), preferred_element_type=F32)


def _dot_tn(a, b):
    return lax.dot_general(a, b, (((0,), (0,)), ((), ())), preferred_element_type=F32)


def _rms(x, eps):
    return x * lax.rsqrt(jnp.mean(x * x, axis=-1, keepdims=True) + eps)


def _silu(x):
    return x * (1.0 / (1.0 + jnp.exp(-x)))


def _const_spec(shape):
    zeros = (0,) * len(shape)
    return pl.BlockSpec(shape, lambda *_: zeros, pipeline_mode=pl.Buffered(1))


def _params(semantics):
    return pltpu.CompilerParams(dimension_semantics=semantics, vmem_limit_bytes=VMEM_LIMIT)


def _ret_proj_kernel(x_ref, g_ref, w_ref, cos_ref, sin_ref, q_ref, k_ref, v_ref, gate_ref):
    d = D_MODEL
    hb = (_rms(x_ref[...], NORM_EPS) * g_ref[...]).astype(BF16)
    c = cos_ref[...]
    s = sin_ref[...]
    half = RET_QK_DIM // 2
    for dst, col0, scale in ((q_ref, 0, 1.0), (k_ref, d, RET_QK_DIM ** -0.5)):
        y = _dot(hb, w_ref[:, col0:col0 + d])
        for h in range(RET_HEADS):
            lo = h * RET_QK_DIM
            y1 = y[:, lo:lo + half]
            y2 = y[:, lo + half:lo + 2 * half]
            dst[:, lo:lo + half] = ((y1 * c - y2 * s) * scale).astype(BF16)
            dst[:, lo + half:lo + 2 * half] = ((y2 * c + y1 * s) * scale).astype(BF16)
    for j in range(2):
        v_ref[:, j * d:(j + 1) * d] = _dot(hb, w_ref[:, (2 + j) * d:(3 + j) * d]).astype(BF16)
        gate_ref[:, j * d:(j + 1) * d] = _silu(
            _dot(hb, w_ref[:, (4 + j) * d:(5 + j) * d])).astype(BF16)


def _ret_proj(x2d, g, w_in, cos, sin, seq):
    t, d = x2d.shape
    tm = min(TOKEN_TILE, seq)
    n_seq = seq // tm
    tok = lambda i: (i, 0)
    pos = lambda i: (i % n_seq, 0)
    return pl.pallas_call(
        _ret_proj_kernel,
        grid=(t // tm,),
        in_specs=[
            pl.BlockSpec((tm, d), tok),
            _const_spec((1, d)),
            _const_spec((d, 6 * d)),
            pl.BlockSpec((tm, LANES), pos),
            pl.BlockSpec((tm, LANES), pos),
        ],
        out_specs=[
            pl.BlockSpec((tm, d), tok),
            pl.BlockSpec((tm, d), tok),
            pl.BlockSpec((tm, 2 * d), tok),
            pl.BlockSpec((tm, 2 * d), tok),
        ],
        out_shape=[
            jax.ShapeDtypeStruct((t, d), BF16),
            jax.ShapeDtypeStruct((t, d), BF16),
            jax.ShapeDtypeStruct((t, 2 * d), BF16),
            jax.ShapeDtypeStruct((t, 2 * d), BF16),
        ],
        compiler_params=_params(("parallel",)),
        name="ret_proj",
    )(x2d, g, w_in, cos, sin)


def _retention_kernel(cd_ref, q_ref, k_ref, v_ref, gate_ref, x_ref, wout_ref,
                      intra_ref, qdec_ref, kdec_ref, o_ref, state_ref, gated_ref):
    c = RET_CHUNK
    n_chunks = q_ref.shape[0] // c

    @pl.when(pl.program_id(1) == 0)
    def _():
        state_ref[...] = jnp.zeros_like(state_ref)

    for h in range(RET_HEADS):
        qk = slice(h * RET_QK_DIM, (h + 1) * RET_QK_DIM)
        vv = slice(h * RET_V_DIM, (h + 1) * RET_V_DIM)
        for n in range(n_chunks):
            rows = slice(n * c, (n + 1) * c)
            qi = q_ref[rows, qk]
            ki = k_ref[rows, qk]
            vi = v_ref[rows, vv]
            scores = _dot_nt(qi, ki) * intra_ref[h]
            intra = _dot(scores.astype(BF16), vi)
            state = state_ref[h]
            inter = _dot(qi, state.astype(BF16)) * qdec_ref[h]
            kd = (ki.astype(F32) * kdec_ref[h]).astype(BF16)
            state_ref[h] = state * cd_ref[h] + _dot_tn(kd, vi)
            o = _rms(intra + inter, NORM_EPS)
            gated_ref[rows, vv] = (gate_ref[rows, vv].astype(F32) * o).astype(BF16)
    o_ref[...] = x_ref[...] + _dot(gated_ref[...], wout_ref[...])


def _retention(chunk_decay, q, k, v, gate, x, w_out, intra_decay, q_decay, k_decay):
    b, s, d = x.shape
    ts = min(TOKEN_TILE, s)
    blk = lambda w: pl.BlockSpec((None, ts, w), lambda bi, si: (bi, si, 0))
    return pl.pallas_call(
        _retention_kernel,
        grid=(b, s // ts),
        in_specs=[
            pl.BlockSpec(memory_space=pltpu.SMEM),
            blk(d), blk(d), blk(2 * d), blk(2 * d), blk(d),
            _const_spec((2 * d, d)),
            _const_spec((RET_HEADS, RET_CHUNK, RET_CHUNK)),
            _const_spec((RET_HEADS, RET_CHUNK, 1)),
            _const_spec((RET_HEADS, RET_CHUNK, 1)),
        ],
        out_specs=blk(d),
        out_shape=jax.ShapeDtypeStruct((b, s, d), F32),
        scratch_shapes=[
            pltpu.VMEM((RET_HEADS, RET_QK_DIM, RET_V_DIM), F32),
            pltpu.VMEM((ts, 2 * d), BF16),
        ],
        compiler_params=_params(("parallel", "arbitrary")),
        name="retention",
    )(chunk_decay, q, k, v, gate, x, w_out, intra_decay, q_decay, k_decay)


def _ffn_kernel(x_ref, g_ref, wg_ref, wu_ref, wd_ref, o_ref, *, f_chunks):
    x = x_ref[...]
    hb = (_rms(x, NORM_EPS) * g_ref[...]).astype(BF16)
    acc = x
    for lo, hi in f_chunks:
        a = _dot(hb, wg_ref[:, lo:hi])
        u = _dot(hb, wu_ref[:, lo:hi])
        acc = acc + _dot((_silu(a) * u).astype(BF16), wd_ref[lo:hi, :])
    o_ref[...] = acc


def _chunks(total, size):
    return tuple((lo, min(lo + size, total)) for lo in range(0, total, size))


def _ffn(x2d, g, wg, wu, wd):
    t, d = x2d.shape
    f = wg.shape[1]
    tm = min(TOKEN_TILE, t)
    tok = lambda i: (i, 0)
    return pl.pallas_call(
        functools.partial(_ffn_kernel, f_chunks=_chunks(f, 1024)),
        grid=(t // tm,),
        in_specs=[
            pl.BlockSpec((tm, d), tok),
            _const_spec((1, d)),
            _const_spec((d, f)),
            _const_spec((d, f)),
            _const_spec((f, d)),
        ],
        out_specs=pl.BlockSpec((tm, d), tok),
        out_shape=jax.ShapeDtypeStruct((t, d), F32),
        compiler_params=_params(("parallel",)),
        name="dense_ffn",
    )(x2d, g, wg, wu, wd)


def _rope64(y, cos, sin_a, sin_b):
    half = DIFF_HEAD_DIM // 2
    return (y * cos + pltpu.roll(y, LANES - half, 1) * sin_a
            + pltpu.roll(y, half, 1) * sin_b)


def _kvq_proj_kernel(x_ref, gkv_ref, gq_ref, wk_ref, wvt_ref, wqt_ref, cos_ref, sa_ref, sb_ref,
                     cost_ref, sint_ref, k_ref, vt_ref, qt_ref):
    d = D_MODEL
    xn = _rms(x_ref[...], NORM_EPS)
    hkv = (xn * gkv_ref[...]).astype(BF16)
    hq = (xn * gq_ref[...]).astype(BF16)
    cos = cos_ref[...]
    sa = sa_ref[...]
    sb = sb_ref[...]
    yk = _dot(hkv, wk_ref[...])
    for j in range(d // LANES):
        cols = slice(j * LANES, (j + 1) * LANES)
        k_ref[:, cols] = _rope64(yk[:, cols], cos, sa, sb).astype(BF16)

    yvt = _dot_nt(wvt_ref[...], hkv)
    for h in range(DIFF_HEADS):
        vt_ref[h] = yvt[h * DIFF_V_DIM:(h + 1) * DIFF_V_DIM, :].astype(BF16)

    yqt = _dot_nt(wqt_ref[...], hq)
    ct = cost_ref[...]
    st = sint_ref[...]
    half = DIFF_HEAD_DIM // 2
    q_scale = DIFF_HEAD_DIM ** -0.5 * LOG2E
    for g in range(d // DIFF_HEAD_DIM):
        r0 = g * DIFF_HEAD_DIM
        a = yqt[r0:r0 + half, :]
        b = yqt[r0 + half:r0 + 2 * half, :]
        qt_ref[r0:r0 + half, :] = ((a * ct - b * st) * q_scale).astype(BF16)
        qt_ref[r0 + half:r0 + 2 * half, :] = ((b * ct + a * st) * q_scale).astype(BF16)


def _kvq_proj(x, gkv, gq, wk, wvt, wqt, cos, sin_a, sin_b, cos_t, sin_t):
    b, s, d = x.shape
    tm = min(TOKEN_TILE, s)
    n_seq = s // tm
    tk = min(ATTN_TILE, s)
    per_blk = tk // tm
    half = DIFF_HEAD_DIM // 2
    pos = lambda bi, i: (i, 0)
    pos_t = lambda bi, i: (0, i)
    return pl.pallas_call(
        _kvq_proj_kernel,
        grid=(b, n_seq),
        in_specs=[
            pl.BlockSpec((None, tm, d), lambda bi, i: (bi, i, 0)),
            _const_spec((1, d)),
            _const_spec((1, d)),
            _const_spec((d, d)),
            _const_spec((d, d)),
            _const_spec((d, d)),
            pl.BlockSpec((tm, LANES), pos),
            pl.BlockSpec((tm, LANES), pos),
            pl.BlockSpec((tm, LANES), pos),
            pl.BlockSpec((half, tm), pos_t),
            pl.BlockSpec((half, tm), pos_t),
        ],
        out_specs=[
            pl.BlockSpec((None, tm, d), lambda bi, i: (bi, i, 0)),
            pl.BlockSpec((None, DIFF_HEADS, None, DIFF_V_DIM, tm),
                         lambda bi, i: (bi, 0, i // per_blk, 0, i % per_blk)),
            pl.BlockSpec((None, d, tm), lambda bi, i: (bi, 0, i)),
        ],
        out_shape=[
            jax.ShapeDtypeStruct((b, s, d), BF16),
            jax.ShapeDtypeStruct((b, DIFF_HEADS, s // tk, DIFF_V_DIM, tk), BF16),
            jax.ShapeDtypeStruct((b, d, s), BF16),
        ],
        compiler_params=_params(("parallel", "parallel")),
        name="kvq_proj",
    )(x, gkv, gq, wk, wvt, wqt, cos, sin_a, sin_b, cos_t, sin_t)


MXU_COLS = 512
MASKED_LOGIT = -1e30
LOGITS_AHEAD = 1


def _attn_kernel(lam_ref, qt_ref, k_ref, vt_ref, g_ref, o_ref, qq_ref, m_ref, l_ref, acc_ref,
                 s_ref, *, out_scale):
    tq = qt_ref.shape[1]
    tk = vt_ref.shape[2]
    assert tq == tk
    r = 2 * tq
    qi = pl.program_id(2)

    qt = qt_ref[...]
    feat = lax.broadcasted_iota(jnp.int32, qt.shape, 0)
    zero = jnp.zeros_like(qt)
    qq_ref[:, 0:tq] = jnp.where(feat < DIFF_HEAD_DIM, qt, zero)
    qq_ref[:, tq:r] = jnp.where(feat >= DIFF_HEAD_DIM, qt, zero)
    m_ref[...] = jnp.full_like(m_ref, MASKED_LOGIT)
    l_ref[...] = jnp.zeros_like(l_ref)
    acc_ref[...] = jnp.zeros_like(acc_ref)

    n_groups = r // MXU_COLS
    ahead = s_ref.shape[0]
    assert n_groups >= ahead

    def keys_needed(c):
        return min(tk, (c * MXU_COLS) % tq + MXU_COLS)

    def logits(j, c, n_keys=tk):
        start = pl.multiple_of(j * tk, tk)
        return _dot(k_ref[pl.ds(start, n_keys), :],
                    qq_ref[:, c * MXU_COLS:(c + 1) * MXU_COLS])

    def block(j, is_diagonal):
        vt = vt_ref[j]
        pending = [s_ref[a] for a in range(ahead)]
        for c in range(n_groups):
            cols = slice(c * MXU_COLS, (c + 1) * MXU_COLS)
            s = pending.pop(0)
            if c + ahead < n_groups:
                n_next = keys_needed(c + ahead) if is_diagonal else tk
                pending.append(logits(j, c + ahead, n_next))
            elif not is_diagonal:
                s_ref[c + ahead - n_groups] = logits(j + 1, c + ahead - n_groups)
            vt_c = vt
            if is_diagonal:
                n_keys = keys_needed(c)
                s = s[0:n_keys, :]
                vt_c = vt[:, 0:n_keys]
                key = lax.broadcasted_iota(jnp.int32, s.shape, 0)
                qry = lax.broadcasted_iota(jnp.int32, s.shape, 1) + (c * MXU_COLS) % tq
                s = jnp.where(key <= qry, s, MASKED_LOGIT)
            m_prev = m_ref[:, cols]
            m_new = jnp.maximum(m_prev, jnp.max(s, axis=0, keepdims=True))
            alpha = jnp.exp2(m_prev - m_new)
            p = jnp.exp2(s - m_new)
            l_ref[:, cols] = alpha * l_ref[:, cols] + jnp.sum(p, axis=0, keepdims=True)
            acc_ref[:, cols] = alpha * acc_ref[:, cols] + _dot(vt_c, p.astype(BF16))
            m_ref[:, cols] = m_new

    def body(j, carry):
        block(j, False)
        return carry

    for a in range(ahead):
        s_ref[a] = logits(0, a)
    lax.fori_loop(0, qi, body, 0)
    block(qi, True)

    o = acc_ref[...] * (1.0 / l_ref[...])
    out = o[:, 0:tq] - lam_ref[0] * o[:, tq:r]
    ms = jnp.mean(out * out, axis=0, keepdims=True)
    out = out * lax.rsqrt(ms + SUBLN_EPS) * g_ref[...] * out_scale
    o_ref[...] = out.T.astype(BF16)


def _diff_attention(lam, qt, k, vt, subln_g_cols, lambda_init):
    b, s, d = k.shape
    tq = vt.shape[-1]
    n_blk = s // tq
    return pl.pallas_call(
        functools.partial(_attn_kernel, out_scale=1.0 - lambda_init),
        grid=(b, DIFF_HEADS, n_blk),
        in_specs=[
            pl.BlockSpec(memory_space=pltpu.SMEM),
            pl.BlockSpec((None, LANES, tq), lambda bi, h, i: (bi, h, i)),
            pl.BlockSpec((None, s, LANES), lambda bi, h, i: (bi, 0, h)),
            pl.BlockSpec((None, None, n_blk, DIFF_V_DIM, tq), lambda bi, h, i: (bi, h, 0, 0, 0)),
            _const_spec((DIFF_V_DIM, tq)),
        ],
        out_specs=pl.BlockSpec((None, tq, LANES), lambda bi, h, i: (bi, i, h)),
        out_shape=jax.ShapeDtypeStruct((b, s, d), BF16),
        scratch_shapes=[
            pltpu.VMEM((LANES, 2 * tq), BF16),
            pltpu.VMEM((1, 2 * tq), F32),
            pltpu.VMEM((1, 2 * tq), F32),
            pltpu.VMEM((DIFF_V_DIM, 2 * tq), F32),
            pltpu.VMEM((LOGITS_AHEAD, tq, MXU_COLS), F32),
        ],
        compiler_params=_params(("parallel", "parallel", "arbitrary")),
        name="diff_attention",
    )(lam, qt, k, vt, subln_g_cols)


def _split_bf16(x):
    hi = x.astype(BF16)
    lo = (x - hi.astype(F32)).astype(BF16)
    return hi, lo


ROUTE_ROWS = 8


def _oproj_router_kernel(a_ref, x_ref, wo_ref, g_ref, wrt_ref, x_out_ref, h_ref, route_ref):
    x = x_ref[...] + _dot(a_ref[...], wo_ref[...])
    x_out_ref[...] = x
    h = _rms(x, NORM_EPS) * g_ref[...]
    h_hi, h_lo = _split_bf16(h)
    h_ref[...] = h_hi
    wrt = wrt_ref[...]
    a = _dot_nt(wrt, h_hi)
    b = _dot_nt(wrt, h_lo)
    ne = N_EXPERTS
    logits = a[0:ne, :] + (b[0:ne, :] + a[ne:2 * ne, :])
    expert = lax.broadcasted_iota(jnp.int32, logits.shape, 0).astype(F32)
    neg = -jnp.inf
    m1 = jnp.max(logits, axis=0, keepdims=True)
    i1 = jnp.min(jnp.where(logits == m1, expert, float(ne)), axis=0, keepdims=True)
    rest = jnp.where(expert == i1, neg, logits)
    m2 = jnp.max(rest, axis=0, keepdims=True)
    i2 = jnp.min(jnp.where(rest == m2, expert, float(ne)), axis=0, keepdims=True)
    e = jnp.exp(m2 - m1)
    w1 = 1.0 / (1.0 + e)
    w2 = e / (1.0 + e)
    rec = lax.broadcasted_iota(jnp.int32, (ROUTE_ROWS, logits.shape[1]), 0)
    route_ref[...] = jnp.where(rec == 0, i1, jnp.where(rec == 1, i2,
                               jnp.where(rec == 2, w1, jnp.where(rec == 3, w2, 0.0))))


def _oproj_router(attn2d, x2d, wo, g, wrt):
    t, d = x2d.shape
    tm = min(TOKEN_TILE, t)
    tok = lambda i: (i, 0)
    return pl.pallas_call(
        _oproj_router_kernel,
        grid=(t // tm,),
        in_specs=[
            pl.BlockSpec((tm, d), tok),
            pl.BlockSpec((tm, d), tok),
            _const_spec((d, d)),
            _const_spec((1, d)),
            _const_spec((2 * N_EXPERTS, d)),
        ],
        out_specs=[
            pl.BlockSpec((tm, d), tok),
            pl.BlockSpec((tm, d), tok),
            pl.BlockSpec((ROUTE_ROWS, tm), lambda i: (0, i)),
        ],
        out_shape=[
            jax.ShapeDtypeStruct((t, d), F32),
            jax.ShapeDtypeStruct((t, d), BF16),
            jax.ShapeDtypeStruct((ROUTE_ROWS, t), F32),
        ],
        compiler_params=_params(("parallel",)),
        name="oproj_router",
    )(attn2d, x2d, wo, g, wrt)


WEIGHT_SLOTS = 2


def _expert_kernel(te_ref, nt_ref, xs_ref, rw_ref, wg_hbm, wu_hbm, wd_hbm, y_prev_hbm, y_ref,
                   wg_ref, wu_ref, wd_ref, sg_ref, su_ref, sd_ref, sem, *, fc):
    del y_prev_hbm
    i = pl.program_id(0)
    e = te_ref[i]
    active = i < nt_ref[0]
    first = jnp.logical_and(active, jnp.logical_or(i == 0, e != te_ref[jnp.maximum(i - 1, 0)]))
    n_chunks = wg_ref.shape[1] // fc

    def chunk_copies(c):
        slot = c % WEIGHT_SLOTS
        cols = pl.ds(c * fc, fc)
        return (
            pltpu.make_async_copy(wg_hbm.at[e, :, cols], sg_ref.at[slot], sem.at[0, slot]),
            pltpu.make_async_copy(wu_hbm.at[e, :, cols], su_ref.at[slot], sem.at[1, slot]),
            pltpu.make_async_copy(wd_hbm.at[e, cols, :], sd_ref.at[slot], sem.at[2, slot]),
        )

    def swiglu_chunk(xs, c):
        lo, hi = c * fc, (c + 1) * fc
        a = _dot(xs, wg_ref[:, lo:hi])
        u = _dot(xs, wu_ref[:, lo:hi])
        return _dot((_silu(a) * u).astype(BF16), wd_ref[lo:hi, :])

    def finish(acc):
        y_ref[...] = (acc * rw_ref[...]).astype(y_ref.dtype)

    @pl.when(first)
    def _():
        for c in range(min(WEIGHT_SLOTS, n_chunks)):
            for cp in chunk_copies(c):
                cp.start()
        xs = xs_ref[...]
        acc = jnp.zeros(y_ref.shape, F32)
        for c in range(n_chunks):
            slot = c % WEIGHT_SLOTS
            lo, hi = c * fc, (c + 1) * fc
            for cp in chunk_copies(c):
                cp.wait()
            wg_ref[:, lo:hi] = sg_ref[slot].astype(BF16)
            wu_ref[:, lo:hi] = su_ref[slot].astype(BF16)
            wd_ref[lo:hi, :] = sd_ref[slot].astype(BF16)
            if c + WEIGHT_SLOTS < n_chunks:
                for cp in chunk_copies(c + WEIGHT_SLOTS):
                    cp.start()
            acc = acc + swiglu_chunk(xs, c)
        finish(acc)

    @pl.when(jnp.logical_and(active, jnp.logical_not(first)))
    def _():
        xs = xs_ref[...]
        acc = jnp.zeros(y_ref.shape, F32)
        for c in range(n_chunks):
            acc = acc + swiglu_chunk(xs, c)
        finish(acc)

    @pl.when(jnp.logical_not(active))
    def _():
        y_ref[...] = jnp.zeros_like(y_ref)


def _experts(tile_expert, n_tiles_used, xs, row_w, wg, wu, wd, y_prev, *, tile_lo):
    p, d = xs.shape
    f = wg.shape[2]
    tm = EXPERT_TILE
    fc = EXPERT_F_CHUNK
    assert f % fc == 0
    n_tiles = p // tm

    def row(i, te, nt):
        return (jnp.maximum(jnp.minimum(i, nt[0] - 1), 0), 0)

    operands = (tile_expert, n_tiles_used, xs, row_w, wg, wu, wd, y_prev)
    grid_spec = pltpu.PrefetchScalarGridSpec(
        num_scalar_prefetch=2,
        grid=(n_tiles,),
        in_specs=[
            pl.BlockSpec((tm, d), row),
            pl.BlockSpec((tm, 1), row),
            pl.BlockSpec(memory_space=pl.ANY),
            pl.BlockSpec(memory_space=pl.ANY),
            pl.BlockSpec(memory_space=pl.ANY),
            pl.BlockSpec(memory_space=pl.ANY),
        ],
        out_specs=pl.BlockSpec((tm, d), lambda i, te, nt: (i + tile_lo, 0)),
        scratch_shapes=[
            pltpu.VMEM((d, f), BF16),
            pltpu.VMEM((d, f), BF16),
            pltpu.VMEM((f, d), BF16),
            pltpu.VMEM((WEIGHT_SLOTS, d, fc), F32),
            pltpu.VMEM((WEIGHT_SLOTS, d, fc), F32),
            pltpu.VMEM((WEIGHT_SLOTS, fc, d), F32),
            pltpu.SemaphoreType.DMA((3, WEIGHT_SLOTS)),
        ],
    )
    return pl.pallas_call(
        functools.partial(_expert_kernel, fc=fc),
        grid_spec=grid_spec,
        out_shape=jax.ShapeDtypeStruct(y_prev.shape, y_prev.dtype),
        input_output_aliases={len(operands) - 1: 0},
        compiler_params=_params(("arbitrary",)),
        name="experts",
    )(*operands)


def _route_tables(route, tm):
    t = route.shape[1]
    n_pairs = t * TOP_K
    n_tiles = n_pairs // tm + N_EXPERTS - 1
    p = n_tiles * tm
    e_pair = route[0:TOP_K, :].astype(jnp.int32).reshape(n_pairs)
    w_pair = route[TOP_K:2 * TOP_K, :].reshape(n_pairs)
    onehot = (e_pair[None, :] == jnp.arange(N_EXPERTS, dtype=jnp.int32)[:, None]).astype(jnp.int32)
    counts = jnp.sum(onehot, axis=1)
    rank = jnp.sum((jnp.cumsum(onehot, axis=1) - onehot) * onehot, axis=0)
    tiles_per = (counts + tm - 1) // tm
    tile_end = jnp.cumsum(tiles_per)
    tile_start = tile_end - tiles_per
    dest = tile_start[e_pair] * tm + rank
    row_pair = jnp.full((p,), n_pairs, jnp.int32).at[dest].set(
        jnp.arange(n_pairs, dtype=jnp.int32), unique_indices=True)
    row_token = jnp.where(row_pair < n_pairs, row_pair % t, 0)
    row_w = jnp.concatenate([w_pair, jnp.zeros((1,), F32)])[row_pair]
    tile_ids = jnp.arange(n_tiles, dtype=jnp.int32)
    tile_expert = jnp.minimum(
        jnp.sum((tile_ids[:, None] >= tile_end[None, :]).astype(jnp.int32), axis=1),
        N_EXPERTS - 1)
    n_used = tile_end[-1:].astype(jnp.int32)
    return row_token, row_w.reshape(p, 1), tile_expert, n_used, dest.reshape(TOP_K, t)


def _final_kernel(x_ref, ya_ref, yb_ref, g_ref, o_ref):
    x = x_ref[...] + (ya_ref[...].astype(F32) + yb_ref[...].astype(F32))
    o_ref[...] = _rms(x, NORM_EPS) * g_ref[...]


def _final(x2d, ya, yb, g):
    t, d = x2d.shape
    tm = min(2 * TOKEN_TILE, t)
    tok = lambda i: (i, 0)
    return pl.pallas_call(
        _final_kernel,
        grid=(t // tm,),
        in_specs=[pl.BlockSpec((tm, d), tok)] * 3 + [_const_spec((1, d))],
        out_specs=pl.BlockSpec((tm, d), tok),
        out_shape=jax.ShapeDtypeStruct((t, d), F32),
        compiler_params=_params(("parallel",)),
        name="final_norm",
    )(x2d, ya, yb, g)


def _rope_angles(seq, dim):
    inv = 1.0 / (ROPE_THETA ** (jnp.arange(0, dim, 2, dtype=F32) / dim))
    return jnp.arange(seq, dtype=F32)[:, None] * inv[None, :]


def _retention_decays():
    h, c = RET_HEADS, RET_CHUNK
    log_gamma = jnp.log1p(-jnp.exp2(-5.0 - jnp.arange(h, dtype=F32)))
    pos = jnp.arange(c, dtype=F32)
    rel = pos[:, None] - pos[None, :]
    intra = jnp.where(rel >= 0, jnp.exp(log_gamma[:, None, None] * jnp.maximum(rel, 0.0)), 0.0)
    q_decay = jnp.exp(log_gamma[:, None] * (pos + 1.0))[:, :, None]
    k_decay = jnp.exp(log_gamma[:, None] * (c - 1.0 - pos))[:, :, None]
    chunk_decay = jnp.exp(log_gamma * c)
    return chunk_decay, intra, q_decay, k_decay


def kernel(x, attn_norm_g, ffn_norm_g, ret_w_in, ret_w_out, kv_norm_g, w_kv, diff_w_q,
           diff_lam_q1, diff_lam_k1, diff_lam_q2, diff_lam_k2, diff_subln_g, diff_w_o,
           ffn_w_gate, ffn_w_up, ffn_w_down, moe_w_router, moe_w_gate, moe_w_up, moe_w_down,
           final_norm_g):
    b, s, d = x.shape
    t = b * s
    row = lambda g: g.reshape(1, -1).astype(F32)
    bf = lambda w: w.astype(BF16)

    ang = _rope_angles(s, RET_QK_DIM)
    q, k, v, gate = _ret_proj(x.reshape(t, d), row(attn_norm_g[0]), bf(ret_w_in[0]),
                              jnp.cos(ang), jnp.sin(ang), s)
    chunk_decay, intra_decay, q_decay, k_decay = _retention_decays()
    x1 = _retention(chunk_decay, q.reshape(b, s, d), k.reshape(b, s, d),
                    v.reshape(b, s, 2 * d), gate.reshape(b, s, 2 * d), x,
                    bf(ret_w_out[0]), intra_decay, q_decay, k_decay)
    x2 = _ffn(x1.reshape(t, d), row(ffn_norm_g[0]), bf(ffn_w_gate[0]), bf(ffn_w_up[0]),
              bf(ffn_w_down[0]))

    ang = _rope_angles(s, DIFF_HEAD_DIM)
    cos32, sin32 = lax.optimization_barrier((jnp.cos(ang), jnp.sin(ang)))
    zero32 = jnp.zeros_like(sin32)
    pairs = LANES // DIFF_HEAD_DIM
    cos = jnp.concatenate([cos32, cos32] * pairs, axis=-1)
    sin_a = jnp.concatenate([-sin32, zero32] * pairs, axis=-1)
    sin_b = jnp.concatenate([zero32, sin32] * pairs, axis=-1)
    ks, vt, qt = _kvq_proj(x2.reshape(b, s, d), row(kv_norm_g), row(attn_norm_g[1]),
                           bf(w_kv[:, :d]), bf(w_kv[:, d:].T), bf(diff_w_q[0].T),
                           cos, sin_a, sin_b, cos32.T, sin32.T)
    lambda_init = 0.8 - 0.6 * math.exp(-0.3 * 1)
    lam = (jnp.exp(jnp.sum(diff_lam_q1[0].astype(F32) * diff_lam_k1[0].astype(F32)))
           - jnp.exp(jnp.sum(diff_lam_q2[0].astype(F32) * diff_lam_k2[0].astype(F32)))
           + lambda_init).reshape(1)
    g_cols = jnp.broadcast_to(diff_subln_g[0].astype(F32)[:, None], (DIFF_V_DIM, vt.shape[-1]))
    attn = _diff_attention(lam, qt, ks, vt, g_cols, lambda_init)

    wr_t = moe_w_router[0].astype(F32).T
    wr_hi = wr_t.astype(BF16)
    wr_lo = (wr_t - wr_hi.astype(F32)).astype(BF16)
    x3, h, route = _oproj_router(attn.reshape(t, d), x2, bf(diff_w_o[0]), row(ffn_norm_g[1]),
                                 jnp.concatenate([wr_hi, wr_lo], axis=0))
    row_token, row_w, tile_expert, n_used, dest = _route_tables(route, EXPERT_TILE)
    tm = EXPERT_TILE
    n_tiles = tile_expert.shape[0]
    n_head = n_tiles // 4
    y = jnp.zeros((n_tiles * tm, d), BF16)
    for lo, hi in ((0, n_head), (n_head, n_tiles)):
        xs = jnp.take(h, row_token[lo * tm:hi * tm], axis=0, mode="clip")
        y = _experts(tile_expert[lo:hi], jnp.clip(n_used - lo, 0, hi - lo), xs,
                     row_w[lo * tm:hi * tm], moe_w_gate[0], moe_w_up[0], moe_w_down[0], y,
                     tile_lo=lo)
    ya = jnp.take(y, dest[0], axis=0, mode="clip")
    yb = jnp.take(y, dest[1], axis=0, mode="clip")
    out = _final(x3, ya, yb, row(final_norm_g))
    return out.reshape(b, s, d)
```

```python
import functools
import math

import jax
import jax.numpy as jnp
from jax import lax
from jax.experimental import pallas as pl
from jax.experimental.pallas import tpu as pltpu

F32 = jnp.float32
BF16 = jnp.bfloat16

D_MODEL = 1024
RET_HEADS = 4
RET_QK_DIM = D_MODEL // RET_HEADS
RET_V_DIM = 2 * D_MODEL // RET_HEADS
RET_CHUNK = 256
DIFF_HEADS = 8
DIFF_HEAD_DIM = D_MODEL // DIFF_HEADS // 2
DIFF_V_DIM = 2 * DIFF_HEAD_DIM
N_EXPERTS = 8
TOP_K = 2
ROPE_THETA = 10000.0
NORM_EPS = 1e-6
SUBLN_EPS = 1e-5
LOG2E = math.log2(math.e)

LANES = 128
VMEM_LIMIT = 56 * 1024 * 1024

TOKEN_TILE = 512
ATTN_TILE = 2048
EXPERT_TILE = 512
EXPERT_F_CHUNK = 512


def _dot(a, b):
    return jnp.dot(a, b, preferred_element_type=F32)


def _dot_nt(a, b):
    return lax.dot_general(a, b, (((1,), (1,)), ((), ())), preferred_element_type=F32)


def _dot_tn(a, b):
    return lax.dot_general(a, b, (((0,), (0,)), ((), ())), preferred_element_type=F32)


def _rms(x, eps):
    return x * lax.rsqrt(jnp.mean(x * x, axis=-1, keepdims=True) + eps)


def _silu(x):
    return x * (1.0 / (1.0 + jnp.exp(-x)))


def _const_spec(shape):
    zeros = (0,) * len(shape)
    return pl.BlockSpec(shape, lambda *_: zeros, pipeline_mode=pl.Buffered(1))


def _params(semantics):
    return pltpu.CompilerParams(dimension_semantics=semantics, vmem_limit_bytes=VMEM_LIMIT)


def _ret_proj_kernel(x_ref, g_ref, w_ref, cos_ref, sin_ref, q_ref, k_ref, v_ref, gate_ref):
    d = D_MODEL
    hb = (_rms(x_ref[...], NORM_EPS) * g_ref[...]).astype(BF16)
    c = cos_ref[...]
    s = sin_ref[...]
    half = RET_QK_DIM // 2
    for dst, col0, scale in ((q_ref, 0, 1.0), (k_ref, d, RET_QK_DIM ** -0.5)):
        y = _dot(hb, w_ref[:, col0:col0 + d])
        for h in range(RET_HEADS):
            lo = h * RET_QK_DIM
            y1 = y[:, lo:lo + half]
            y2 = y[:, lo + half:lo + 2 * half]
            dst[:, lo:lo + half] = ((y1 * c - y2 * s) * scale).astype(BF16)
            dst[:, lo + half:lo + 2 * half] = ((y2 * c + y1 * s) * scale).astype(BF16)
    for j in range(2):
        v_ref[:, j * d:(j + 1) * d] = _dot(hb, w_ref[:, (2 + j) * d:(3 + j) * d]).astype(BF16)
        gate_ref[:, j * d:(j + 1) * d] = _silu(
            _dot(hb, w_ref[:, (4 + j) * d:(5 + j) * d])).astype(BF16)


def _ret_proj(x2d, g, w_in, cos, sin, seq):
    t, d = x2d.shape
    tm = min(TOKEN_TILE, seq)
    n_seq = seq // tm
    tok = lambda i: (i, 0)
    pos = lambda i: (i % n_seq, 0)
    return pl.pallas_call(
        _ret_proj_kernel,
        grid=(t // tm,),
        in_specs=[
            pl.BlockSpec((tm, d), tok),
            _const_spec((1, d)),
            _const_spec((d, 6 * d)),
            pl.BlockSpec((tm, LANES), pos),
            pl.BlockSpec((tm, LANES), pos),
        ],
        out_specs=[
            pl.BlockSpec((tm, d), tok),
            pl.BlockSpec((tm, d), tok),
            pl.BlockSpec((tm, 2 * d), tok),
            pl.BlockSpec((tm, 2 * d), tok),
        ],
        out_shape=[
            jax.ShapeDtypeStruct((t, d), BF16),
            jax.ShapeDtypeStruct((t, d), BF16),
            jax.ShapeDtypeStruct((t, 2 * d), BF16),
            jax.ShapeDtypeStruct((t, 2 * d), BF16),
        ],
        compiler_params=_params(("parallel",)),
        name="ret_proj",
    )(x2d, g, w_in, cos, sin)


def _retention_kernel(cd_ref, q_ref, k_ref, v_ref, gate_ref, x_ref, wout_ref,
                      intra_ref, qdec_ref, kdec_ref, o_ref, state_ref, gated_ref):
    c = RET_CHUNK
    n_chunks = q_ref.shape[0] // c

    @pl.when(pl.program_id(1) == 0)
    def _():
        state_ref[...] = jnp.zeros_like(state_ref)

    for h in range(RET_HEADS):
        qk = slice(h * RET_QK_DIM, (h + 1) * RET_QK_DIM)
        vv = slice(h * RET_V_DIM, (h + 1) * RET_V_DIM)
        for n in range(n_chunks):
            rows = slice(n * c, (n + 1) * c)
            qi = q_ref[rows, qk]
            ki = k_ref[rows, qk]
            vi = v_ref[rows, vv]
            scores = _dot_nt(qi, ki) * intra_ref[h]
            intra = _dot(scores.astype(BF16), vi)
            state = state_ref[h]
            inter = _dot(qi, state.astype(BF16)) * qdec_ref[h]
            kd = (ki.astype(F32) * kdec_ref[h]).astype(BF16)
            state_ref[h] = state * cd_ref[h] + _dot_tn(kd, vi)
            o = _rms(intra + inter, NORM_EPS)
            gated_ref[rows, vv] = (gate_ref[rows, vv].astype(F32) * o).astype(BF16)
    o_ref[...] = x_ref[...] + _dot(gated_ref[...], wout_ref[...])


def _retention(chunk_decay, q, k, v, gate, x, w_out, intra_decay, q_decay, k_decay):
    b, s, d = x.shape
    ts = min(TOKEN_TILE, s)
    blk = lambda w: pl.BlockSpec((None, ts, w), lambda bi, si: (bi, si, 0))
    return pl.pallas_call(
        _retention_kernel,
        grid=(b, s // ts),
        in_specs=[
            pl.BlockSpec(memory_space=pltpu.SMEM),
            blk(d), blk(d), blk(2 * d), blk(2 * d), blk(d),
            _const_spec((2 * d, d)),
            _const_spec((RET_HEADS, RET_CHUNK, RET_CHUNK)),
            _const_spec((RET_HEADS, RET_CHUNK, 1)),
            _const_spec((RET_HEADS, RET_CHUNK, 1)),
        ],
        out_specs=blk(d),
        out_shape=jax.ShapeDtypeStruct((b, s, d), F32),
        scratch_shapes=[
            pltpu.VMEM((RET_HEADS, RET_QK_DIM, RET_V_DIM), F32),
            pltpu.VMEM((ts, 2 * d), BF16),
        ],
        compiler_params=_params(("parallel", "arbitrary")),
        name="retention",
    )(chunk_decay, q, k, v, gate, x, w_out, intra_decay, q_decay, k_decay)


def _ffn_kernel(x_ref, g_ref, wg_ref, wu_ref, wd_ref, o_ref, *, f_chunks):
    x = x_ref[...]
    hb = (_rms(x, NORM_EPS) * g_ref[...]).astype(BF16)
    acc = x
    for lo, hi in f_chunks:
        a = _dot(hb, wg_ref[:, lo:hi])
        u = _dot(hb, wu_ref[:, lo:hi])
        acc = acc + _dot((_silu(a) * u).astype(BF16), wd_ref[lo:hi, :])
    o_ref[...] = acc


def _chunks(total, size):
    return tuple((lo, min(lo + size, total)) for lo in range(0, total, size))


def _ffn(x2d, g, wg, wu, wd):
    t, d = x2d.shape
    f = wg.shape[1]
    tm = min(TOKEN_TILE, t)
    tok = lambda i: (i, 0)
    return pl.pallas_call(
        functools.partial(_ffn_kernel, f_chunks=_chunks(f, 1024)),
        grid=(t // tm,),
        in_specs=[
            pl.BlockSpec((tm, d), tok),
            _const_spec((1, d)),
            _const_spec((d, f)),
            _const_spec((d, f)),
            _const_spec((f, d)),
        ],
        out_specs=pl.BlockSpec((tm, d), tok),
        out_shape=jax.ShapeDtypeStruct((t, d), F32),
        compiler_params=_params(("parallel",)),
        name="dense_ffn",
    )(x2d, g, wg, wu, wd)


def _rope64(y, cos, sin_a, sin_b):
    half = DIFF_HEAD_DIM // 2
    return (y * cos + pltpu.roll(y, LANES - half, 1) * sin_a
            + pltpu.roll(y, half, 1) * sin_b)


def _kvq_proj_kernel(x_ref, gkv_ref, gq_ref, wk_ref, wvt_ref, wqt_ref, cos_ref, sa_ref, sb_ref,
                     cost_ref, sint_ref, k_ref, vt_ref, qt_ref):
    d = D_MODEL
    xn = _rms(x_ref[...], NORM_EPS)
    hkv = (xn * gkv_ref[...]).astype(BF16)
    hq = (xn * gq_ref[...]).astype(BF16)
    cos = cos_ref[...]
    sa = sa_ref[...]
    sb = sb_ref[...]
    yk = _dot(hkv, wk_ref[...])
    for j in range(d // LANES):
        cols = slice(j * LANES, (j + 1) * LANES)
        k_ref[:, cols] = _rope64(yk[:, cols], cos, sa, sb).astype(BF16)

    yvt = _dot_nt(wvt_ref[...], hkv)
    for h in range(DIFF_HEADS):
        vt_ref[h] = yvt[h * DIFF_V_DIM:(h + 1) * DIFF_V_DIM, :].astype(BF16)

    yqt = _dot_nt(wqt_ref[...], hq)
    ct = cost_ref[...]
    st = sint_ref[...]
    half = DIFF_HEAD_DIM // 2
    q_scale = DIFF_HEAD_DIM ** -0.5 * LOG2E
    for g in range(d // DIFF_HEAD_DIM):
        r0 = g * DIFF_HEAD_DIM
        a = yqt[r0:r0 + half, :]
        b = yqt[r0 + half:r0 + 2 * half, :]
        qt_ref[r0:r0 + half, :] = ((a * ct - b * st) * q_scale).astype(BF16)
        qt_ref[r0 + half:r0 + 2 * half, :] = ((b * ct + a * st) * q_scale).astype(BF16)


def _kvq_proj(x, gkv, gq, wk, wvt, wqt, cos, sin_a, sin_b, cos_t, sin_t):
    b, s, d = x.shape
    tm = min(TOKEN_TILE, s)
    n_seq = s // tm
    tk = min(ATTN_TILE, s)
    per_blk = tk // tm
    half = DIFF_HEAD_DIM // 2
    pos = lambda bi, i: (i, 0)
    pos_t = lambda bi, i: (0, i)
    return pl.pallas_call(
        _kvq_proj_kernel,
        grid=(b, n_seq),
        in_specs=[
            pl.BlockSpec((None, tm, d), lambda bi, i: (bi, i, 0)),
            _const_spec((1, d)),
            _const_spec((1, d)),
            _const_spec((d, d)),
            _const_spec((d, d)),
            _const_spec((d, d)),
            pl.BlockSpec((tm, LANES), pos),
            pl.BlockSpec((tm, LANES), pos),
            pl.BlockSpec((tm, LANES), pos),
            pl.BlockSpec((half, tm), pos_t),
            pl.BlockSpec((half, tm), pos_t),
        ],
        out_specs=[
            pl.BlockSpec((None, tm, d), lambda bi, i: (bi, i, 0)),
            pl.BlockSpec((None, DIFF_HEADS, None, DIFF_V_DIM, tm),
                         lambda bi, i: (bi, 0, i // per_blk, 0, i % per_blk)),
            pl.BlockSpec((None, d, tm), lambda bi, i: (bi, 0, i)),
        ],
        out_shape=[
            jax.ShapeDtypeStruct((b, s, d), BF16),
            jax.ShapeDtypeStruct((b, DIFF_HEADS, s // tk, DIFF_V_DIM, tk), BF16),
            jax.ShapeDtypeStruct((b, d, s), BF16),
        ],
        compiler_params=_params(("parallel", "parallel")),
        name="kvq_proj",
    )(x, gkv, gq, wk, wvt, wqt, cos, sin_a, sin_b, cos_t, sin_t)


MXU_COLS = 512
MASKED_LOGIT = -1e30
LOGITS_AHEAD = 1


def _attn_kernel(lam_ref, qt_ref, k_ref, vt_ref, g_ref, o_ref, qq_ref, m_ref, l_ref, acc_ref,
                 s_ref, *, out_scale):
    tq = qt_ref.shape[1]
    tk = vt_ref.shape[2]
    assert tq == tk
    r = 2 * tq
    qi = pl.program_id(2)

    qt = qt_ref[...]
    feat = lax.broadcasted_iota(jnp.int32, qt.shape, 0)
    zero = jnp.zeros_like(qt)
    qq_ref[:, 0:tq] = jnp.where(feat < DIFF_HEAD_DIM, qt, zero)
    qq_ref[:, tq:r] = jnp.where(feat >= DIFF_HEAD_DIM, qt, zero)
    m_ref[...] = jnp.full_like(m_ref, MASKED_LOGIT)
    l_ref[...] = jnp.zeros_like(l_ref)
    acc_ref[...] = jnp.zeros_like(acc_ref)

    n_groups = r // MXU_COLS
    ahead = s_ref.shape[0]
    assert n_groups >= ahead

    def keys_needed(c):
        return min(tk, (c * MXU_COLS) % tq + MXU_COLS)

    def logits(j, c, n_keys=tk):
        start = pl.multiple_of(j * tk, tk)
        return _dot(k_ref[pl.ds(start, n_keys), :],
                    qq_ref[:, c * MXU_COLS:(c + 1) * MXU_COLS])

    def block(j, is_diagonal):
        vt = vt_ref[j]
        pending = [s_ref[a] for a in range(ahead)]
        for c in range(n_groups):
            cols = slice(c * MXU_COLS, (c + 1) * MXU_COLS)
            s = pending.pop(0)
            if c + ahead < n_groups:
                n_next = keys_needed(c + ahead) if is_diagonal else tk
                pending.append(logits(j, c + ahead, n_next))
            elif not is_diagonal:
                s_ref[c + ahead - n_groups] = logits(j + 1, c + ahead - n_groups)
            vt_c = vt
            if is_diagonal:
                n_keys = keys_needed(c)
                s = s[0:n_keys, :]
                vt_c = vt[:, 0:n_keys]
                key = lax.broadcasted_iota(jnp.int32, s.shape, 0)
                qry = lax.broadcasted_iota(jnp.int32, s.shape, 1) + (c * MXU_COLS) % tq
                s = jnp.where(key <= qry, s, MASKED_LOGIT)
            m_prev = m_ref[:, cols]
            m_new = jnp.maximum(m_prev, jnp.max(s, axis=0, keepdims=True))
            alpha = jnp.exp2(m_prev - m_new)
            p = jnp.exp2(s - m_new)
            l_ref[:, cols] = alpha * l_ref[:, cols] + jnp.sum(p, axis=0, keepdims=True)
            acc_ref[:, cols] = alpha * acc_ref[:, cols] + _dot(vt_c, p.astype(BF16))
            m_ref[:, cols] = m_new

    def body(j, carry):
        block(j, False)
        return carry

    for a in range(ahead):
        s_ref[a] = logits(0, a)
    lax.fori_loop(0, qi, body, 0)
    block(qi, True)

    o = acc_ref[...] * (1.0 / l_ref[...])
    out = o[:, 0:tq] - lam_ref[0] * o[:, tq:r]
    ms = jnp.mean(out * out, axis=0, keepdims=True)
    out = out * lax.rsqrt(ms + SUBLN_EPS) * g_ref[...] * out_scale
    o_ref[...] = out.T.astype(BF16)


def _diff_attention(lam, qt, k, vt, subln_g_cols, lambda_init):
    b, s, d = k.shape
    tq = vt.shape[-1]
    n_blk = s // tq
    return pl.pallas_call(
        functools.partial(_attn_kernel, out_scale=1.0 - lambda_init),
        grid=(b, DIFF_HEADS, n_blk),
        in_specs=[
            pl.BlockSpec(memory_space=pltpu.SMEM),
            pl.BlockSpec((None, LANES, tq), lambda bi, h, i: (bi, h, i)),
            pl.BlockSpec((None, s, LANES), lambda bi, h, i: (bi, 0, h)),
            pl.BlockSpec((None, None, n_blk, DIFF_V_DIM, tq), lambda bi, h, i: (bi, h, 0, 0, 0)),
            _const_spec((DIFF_V_DIM, tq)),
        ],
        out_specs=pl.BlockSpec((None, tq, LANES), lambda bi, h, i: (bi, i, h)),
        out_shape=jax.ShapeDtypeStruct((b, s, d), BF16),
        scratch_shapes=[
            pltpu.VMEM((LANES, 2 * tq), BF16),
            pltpu.VMEM((1, 2 * tq), F32),
            pltpu.VMEM((1, 2 * tq), F32),
            pltpu.VMEM((DIFF_V_DIM, 2 * tq), F32),
            pltpu.VMEM((LOGITS_AHEAD, tq, MXU_COLS), F32),
        ],
        compiler_params=_params(("parallel", "parallel", "arbitrary")),
        name="diff_attention",
    )(lam, qt, k, vt, subln_g_cols)


def _split_bf16(x):
    hi = x.astype(BF16)
    lo = (x - hi.astype(F32)).astype(BF16)
    return hi, lo


ROUTE_ROWS = 8


def _oproj_router_kernel(a_ref, x_ref, wo_ref, g_ref, wrt_ref, x_out_ref, h_ref, route_ref):
    x = x_ref[...] + _dot(a_ref[...], wo_ref[...])
    x_out_ref[...] = x
    h = _rms(x, NORM_EPS) * g_ref[...]
    h_hi, h_lo = _split_bf16(h)
    h_ref[...] = h_hi
    wrt = wrt_ref[...]
    a = _dot_nt(wrt, h_hi)
    b = _dot_nt(wrt, h_lo)
    ne = N_EXPERTS
    logits = a[0:ne, :] + (b[0:ne, :] + a[ne:2 * ne, :])
    expert = lax.broadcasted_iota(jnp.int32, logits.shape, 0).astype(F32)
    neg = -jnp.inf
    m1 = jnp.max(logits, axis=0, keepdims=True)
    i1 = jnp.min(jnp.where(logits == m1, expert, float(ne)), axis=0, keepdims=True)
    rest = jnp.where(expert == i1, neg, logits)
    m2 = jnp.max(rest, axis=0, keepdims=True)
    i2 = jnp.min(jnp.where(rest == m2, expert, float(ne)), axis=0, keepdims=True)
    e = jnp.exp(m2 - m1)
    w1 = 1.0 / (1.0 + e)
    w2 = e / (1.0 + e)
    rec = lax.broadcasted_iota(jnp.int32, (ROUTE_ROWS, logits.shape[1]), 0)
    route_ref[...] = jnp.where(rec == 0, i1, jnp.where(rec == 1, i2,
                               jnp.where(rec == 2, w1, jnp.where(rec == 3, w2, 0.0))))


def _oproj_router(attn2d, x2d, wo, g, wrt):
    t, d = x2d.shape
    tm = min(TOKEN_TILE, t)
    tok = lambda i: (i, 0)
    return pl.pallas_call(
        _oproj_router_kernel,
        grid=(t // tm,),
        in_specs=[
            pl.BlockSpec((tm, d), tok),
            pl.BlockSpec((tm, d), tok),
            _const_spec((d, d)),
            _const_spec((1, d)),
            _const_spec((2 * N_EXPERTS, d)),
        ],
        out_specs=[
            pl.BlockSpec((tm, d), tok),
            pl.BlockSpec((tm, d), tok),
            pl.BlockSpec((ROUTE_ROWS, tm), lambda i: (0, i)),
        ],
        out_shape=[
            jax.ShapeDtypeStruct((t, d), F32),
            jax.ShapeDtypeStruct((t, d), BF16),
            jax.ShapeDtypeStruct((ROUTE_ROWS, t), F32),
        ],
        compiler_params=_params(("parallel",)),
        name="oproj_router",
    )(attn2d, x2d, wo, g, wrt)


WEIGHT_SLOTS = 2


def _expert_kernel(te_ref, nt_ref, xs_ref, rw_ref, wg_hbm, wu_hbm, wd_hbm, y_prev_hbm, y_ref,
                   wg_ref, wu_ref, wd_ref, sg_ref, su_ref, sd_ref, sem, *, fc):
    del y_prev_hbm
    i = pl.program_id(0)
    e = te_ref[i]
    active = i < nt_ref[0]
    first = jnp.logical_and(active, jnp.logical_or(i == 0, e != te_ref[jnp.maximum(i - 1, 0)]))
    n_chunks = wg_ref.shape[1] // fc

    def chunk_copies(c):
        slot = c % WEIGHT_SLOTS
        cols = pl.ds(c * fc, fc)
        return (
            pltpu.make_async_copy(wg_hbm.at[e, :, cols], sg_ref.at[slot], sem.at[0, slot]),
            pltpu.make_async_copy(wu_hbm.at[e, :, cols], su_ref.at[slot], sem.at[1, slot]),
            pltpu.make_async_copy(wd_hbm.at[e, cols, :], sd_ref.at[slot], sem.at[2, slot]),
        )

    def swiglu_chunk(xs, c):
        lo, hi = c * fc, (c + 1) * fc
        a = _dot(xs, wg_ref[:, lo:hi])
        u = _dot(xs, wu_ref[:, lo:hi])
        return _dot((_silu(a) * u).astype(BF16), wd_ref[lo:hi, :])

    def finish(acc):
        y_ref[...] = (acc * rw_ref[...]).astype(y_ref.dtype)

    @pl.when(first)
    def _():
        for c in range(min(WEIGHT_SLOTS, n_chunks)):
            for cp in chunk_copies(c):
                cp.start()
        xs = xs_ref[...]
        acc = jnp.zeros(y_ref.shape, F32)
        for c in range(n_chunks):
            slot = c % WEIGHT_SLOTS
            lo, hi = c * fc, (c + 1) * fc
            for cp in chunk_copies(c):
                cp.wait()
            wg_ref[:, lo:hi] = sg_ref[slot].astype(BF16)
            wu_ref[:, lo:hi] = su_ref[slot].astype(BF16)
            wd_ref[lo:hi, :] = sd_ref[slot].astype(BF16)
            if c + WEIGHT_SLOTS < n_chunks:
                for cp in chunk_copies(c + WEIGHT_SLOTS):
                    cp.start()
            acc = acc + swiglu_chunk(xs, c)
        finish(acc)

    @pl.when(jnp.logical_and(active, jnp.logical_not(first)))
    def _():
        xs = xs_ref[...]
        acc = jnp.zeros(y_ref.shape, F32)
        for c in range(n_chunks):
            acc = acc + swiglu_chunk(xs, c)
        finish(acc)

    @pl.when(jnp.logical_not(active))
    def _():
        y_ref[...] = jnp.zeros_like(y_ref)


def _experts(tile_expert, n_tiles_used, xs, row_w, wg, wu, wd, y_prev, *, tile_lo):
    p, d = xs.shape
    f = wg.shape[2]
    tm = EXPERT_TILE
    fc = EXPERT_F_CHUNK
    assert f % fc == 0
    n_tiles = p // tm

    def row(i, te, nt):
        return (jnp.maximum(jnp.minimum(i, nt[0] - 1), 0), 0)

    operands = (tile_expert, n_tiles_used, xs, row_w, wg, wu, wd, y_prev)
    grid_spec = pltpu.PrefetchScalarGridSpec(
        num_scalar_prefetch=2,
        grid=(n_tiles,),
        in_specs=[
            pl.BlockSpec((tm, d), row),
            pl.BlockSpec((tm, 1), row),
            pl.BlockSpec(memory_space=pl.ANY),
            pl.BlockSpec(memory_space=pl.ANY),
            pl.BlockSpec(memory_space=pl.ANY),
            pl.BlockSpec(memory_space=pl.ANY),
        ],
        out_specs=pl.BlockSpec((tm, d), lambda i, te, nt: (i + tile_lo, 0)),
        scratch_shapes=[
            pltpu.VMEM((d, f), BF16),
            pltpu.VMEM((d, f), BF16),
            pltpu.VMEM((f, d), BF16),
            pltpu.VMEM((WEIGHT_SLOTS, d, fc), F32),
            pltpu.VMEM((WEIGHT_SLOTS, d, fc), F32),
            pltpu.VMEM((WEIGHT_SLOTS, fc, d), F32),
            pltpu.SemaphoreType.DMA((3, WEIGHT_SLOTS)),
        ],
    )
    return pl.pallas_call(
        functools.partial(_expert_kernel, fc=fc),
        grid_spec=grid_spec,
        out_shape=jax.ShapeDtypeStruct(y_prev.shape, y_prev.dtype),
        input_output_aliases={len(operands) - 1: 0},
        compiler_params=_params(("arbitrary",)),
        name="experts",
    )(*operands)


def _route_tables(route, tm):
    t = route.shape[1]
    n_pairs = t * TOP_K
    n_tiles = n_pairs // tm + N_EXPERTS - 1
    p = n_tiles * tm
    e_pair = route[0:TOP_K, :].astype(jnp.int32).reshape(n_pairs)
    w_pair = route[TOP_K:2 * TOP_K, :].reshape(n_pairs)
    onehot = (e_pair[None, :] == jnp.arange(N_EXPERTS, dtype=jnp.int32)[:, None]).astype(jnp.int32)
    counts = jnp.sum(onehot, axis=1)
    rank = jnp.sum((jnp.cumsum(onehot, axis=1) - onehot) * onehot, axis=0)
    tiles_per = (counts + tm - 1) // tm
    tile_end = jnp.cumsum(tiles_per)
    tile_start = tile_end - tiles_per
    dest = tile_start[e_pair] * tm + rank
    tile_ids = jnp.arange(n_tiles, dtype=jnp.int32)
    tile_expert = jnp.minimum(
        jnp.sum((tile_ids[:, None] >= tile_end[None, :]).astype(jnp.int32), axis=1),
        N_EXPERTS - 1)
    n_used = tile_end[-1:].astype(jnp.int32)
    pair_ids = jnp.arange(n_pairs, dtype=jnp.int32)
    by_expert = jnp.sort(e_pair * n_pairs + pair_ids) % n_pairs
    group_first = jnp.cumsum(counts) - counts
    row_in_group = tile_ids * tm - tile_start[tile_expert] * tm
    slot = (group_first[tile_expert] + row_in_group)[:, None] + jnp.arange(tm, dtype=jnp.int32)[None, :]
    valid = (row_in_group[:, None] + jnp.arange(tm, dtype=jnp.int32)[None, :]
             < counts[tile_expert][:, None])
    row_pair = jnp.where(valid, by_expert[jnp.clip(slot, 0, n_pairs - 1)], n_pairs).reshape(p)
    row_token = jnp.where(row_pair < n_pairs, row_pair % t, 0)
    row_w = jnp.concatenate([w_pair, jnp.zeros((1,), F32)])[row_pair]
    return row_token, row_w.reshape(p, 1), tile_expert, n_used, dest


def _final_kernel(x_ref, ya_ref, yb_ref, g_ref, o_ref):
    x = x_ref[...] + (ya_ref[...].astype(F32) + yb_ref[...].astype(F32))
    o_ref[...] = _rms(x, NORM_EPS) * g_ref[...]


def _final(x2d, y_picks, g):
    t, d = x2d.shape
    tm = min(2 * TOKEN_TILE, t)
    n_steps = t // tm
    tok = lambda i: (i, 0)
    return pl.pallas_call(
        _final_kernel,
        grid=(n_steps,),
        in_specs=[
            pl.BlockSpec((tm, d), tok),
            pl.BlockSpec((tm, d), tok),
            pl.BlockSpec((tm, d), lambda i: (i + n_steps, 0)),
            _const_spec((1, d)),
        ],
        out_specs=pl.BlockSpec((tm, d), tok),
        out_shape=jax.ShapeDtypeStruct((t, d), F32),
        compiler_params=_params(("parallel",)),
        name="final_norm",
    )(x2d, y_picks, y_picks, g)


def _rope_angles(seq, dim):
    inv = 1.0 / (ROPE_THETA ** (jnp.arange(0, dim, 2, dtype=F32) / dim))
    return jnp.arange(seq, dtype=F32)[:, None] * inv[None, :]


def _retention_decays():
    h, c = RET_HEADS, RET_CHUNK
    log_gamma = jnp.log1p(-jnp.exp2(-5.0 - jnp.arange(h, dtype=F32)))
    pos = jnp.arange(c, dtype=F32)
    rel = pos[:, None] - pos[None, :]
    intra = jnp.where(rel >= 0, jnp.exp(log_gamma[:, None, None] * jnp.maximum(rel, 0.0)), 0.0)
    q_decay = jnp.exp(log_gamma[:, None] * (pos + 1.0))[:, :, None]
    k_decay = jnp.exp(log_gamma[:, None] * (c - 1.0 - pos))[:, :, None]
    chunk_decay = jnp.exp(log_gamma * c)
    return chunk_decay, intra, q_decay, k_decay


def kernel(x, attn_norm_g, ffn_norm_g, ret_w_in, ret_w_out, kv_norm_g, w_kv, diff_w_q,
           diff_lam_q1, diff_lam_k1, diff_lam_q2, diff_lam_k2, diff_subln_g, diff_w_o,
           ffn_w_gate, ffn_w_up, ffn_w_down, moe_w_router, moe_w_gate, moe_w_up, moe_w_down,
           final_norm_g):
    b, s, d = x.shape
    t = b * s
    row = lambda g: g.reshape(1, -1).astype(F32)
    bf = lambda w: w.astype(BF16)

    ang = _rope_angles(s, RET_QK_DIM)
    q, k, v, gate = _ret_proj(x.reshape(t, d), row(attn_norm_g[0]), bf(ret_w_in[0]),
                              jnp.cos(ang), jnp.sin(ang), s)
    chunk_decay, intra_decay, q_decay, k_decay = _retention_decays()
    x1 = _retention(chunk_decay, q.reshape(b, s, d), k.reshape(b, s, d),
                    v.reshape(b, s, 2 * d), gate.reshape(b, s, 2 * d), x,
                    bf(ret_w_out[0]), intra_decay, q_decay, k_decay)
    x2 = _ffn(x1.reshape(t, d), row(ffn_norm_g[0]), bf(ffn_w_gate[0]), bf(ffn_w_up[0]),
              bf(ffn_w_down[0]))

    ang = _rope_angles(s, DIFF_HEAD_DIM)
    cos32, sin32 = lax.optimization_barrier((jnp.cos(ang), jnp.sin(ang)))
    zero32 = jnp.zeros_like(sin32)
    pairs = LANES // DIFF_HEAD_DIM
    cos = jnp.concatenate([cos32, cos32] * pairs, axis=-1)
    sin_a = jnp.concatenate([-sin32, zero32] * pairs, axis=-1)
    sin_b = jnp.concatenate([zero32, sin32] * pairs, axis=-1)
    ks, vt, qt = _kvq_proj(x2.reshape(b, s, d), row(kv_norm_g), row(attn_norm_g[1]),
                           bf(w_kv[:, :d]), bf(w_kv[:, d:].T), bf(diff_w_q[0].T),
                           cos, sin_a, sin_b, cos32.T, sin32.T)
    lambda_init = 0.8 - 0.6 * math.exp(-0.3 * 1)
    lam = (jnp.exp(jnp.sum(diff_lam_q1[0].astype(F32) * diff_lam_k1[0].astype(F32)))
           - jnp.exp(jnp.sum(diff_lam_q2[0].astype(F32) * diff_lam_k2[0].astype(F32)))
           + lambda_init).reshape(1)
    g_cols = jnp.broadcast_to(diff_subln_g[0].astype(F32)[:, None], (DIFF_V_DIM, vt.shape[-1]))
    attn = _diff_attention(lam, qt, ks, vt, g_cols, lambda_init)

    wr_t = moe_w_router[0].astype(F32).T
    wr_hi = wr_t.astype(BF16)
    wr_lo = (wr_t - wr_hi.astype(F32)).astype(BF16)
    x3, h, route = _oproj_router(attn.reshape(t, d), x2, bf(diff_w_o[0]), row(ffn_norm_g[1]),
                                 jnp.concatenate([wr_hi, wr_lo], axis=0))
    row_token, row_w, tile_expert, n_used, dest = _route_tables(route, EXPERT_TILE)
    tm = EXPERT_TILE
    n_tiles = tile_expert.shape[0]
    n_head = n_tiles // 4
    y = jnp.zeros((n_tiles * tm, d), BF16)
    for lo, hi in ((0, n_head), (n_head, n_tiles)):
        xs = jnp.take(h, row_token[lo * tm:hi * tm], axis=0, mode="clip")
        y = _experts(tile_expert[lo:hi], jnp.clip(n_used - lo, 0, hi - lo), xs,
                     row_w[lo * tm:hi * tm], moe_w_gate[0], moe_w_up[0], moe_w_down[0], y,
                     tile_lo=lo)
    y_picks = jnp.take(y, dest, axis=0, mode="clip")
    out = _final(x3, y_picks, row(final_norm_g))
    return out.reshape(b, s, d)
```

```python
import functools
import math

import jax
import jax.numpy as jnp
from jax import lax
from jax.experimental import pallas as pl
from jax.experimental.pallas import tpu as pltpu

F32 = jnp.float32
BF16 = jnp.bfloat16

D_MODEL = 1024
RET_HEADS = 4
RET_QK_DIM = D_MODEL // RET_HEADS
RET_V_DIM = 2 * D_MODEL // RET_HEADS
RET_CHUNK = 256
DIFF_HEADS = 8
DIFF_HEAD_DIM = D_MODEL // DIFF_HEADS // 2
DIFF_V_DIM = 2 * DIFF_HEAD_DIM
N_EXPERTS = 8
TOP_K = 2
ROPE_THETA = 10000.0
NORM_EPS = 1e-6
SUBLN_EPS = 1e-5
LOG2E = math.log2(math.e)

LANES = 128
VMEM_LIMIT = 56 * 1024 * 1024

TOKEN_TILE = 512
ATTN_TILE = 2048
EXPERT_TILE = 512
EXPERT_F_CHUNK = 512


def _dot(a, b):
    return jnp.dot(a, b, preferred_element_type=F32)


def _dot_nt(a, b):
    return lax.dot_general(a, b, (((1,), (1,)), ((), ())), preferred_element_type=F32)


def _dot_tn(a, b):
    return lax.dot_general(a, b, (((0,), (0,)), ((), ())), preferred_element_type=F32)


def _rms(x, eps):
    return x * lax.rsqrt(jnp.mean(x * x, axis=-1, keepdims=True) + eps)


def _silu(x):
    return x * (1.0 / (1.0 + jnp.exp(-x)))


def _const_spec(shape):
    zeros = (0,) * len(shape)
    return pl.BlockSpec(shape, lambda *_: zeros, pipeline_mode=pl.Buffered(1))


def _params(semantics):
    return pltpu.CompilerParams(dimension_semantics=semantics, vmem_limit_bytes=VMEM_LIMIT)


def _ret_proj_kernel(x_ref, g_ref, w_ref, cos_ref, sin_ref, q_ref, k_ref, v_ref, gate_ref):
    d = D_MODEL
    hb = (_rms(x_ref[...], NORM_EPS) * g_ref[...]).astype(BF16)
    c = cos_ref[...]
    s = sin_ref[...]
    half = RET_QK_DIM // 2
    for dst, col0, scale in ((q_ref, 0, 1.0), (k_ref, d, RET_QK_DIM ** -0.5)):
        y = _dot(hb, w_ref[:, col0:col0 + d])
        for h in range(RET_HEADS):
            lo = h * RET_QK_DIM
            y1 = y[:, lo:lo + half]
            y2 = y[:, lo + half:lo + 2 * half]
            dst[:, lo:lo + half] = ((y1 * c - y2 * s) * scale).astype(BF16)
            dst[:, lo + half:lo + 2 * half] = ((y2 * c + y1 * s) * scale).astype(BF16)
    for j in range(2):
        v_ref[:, j * d:(j + 1) * d] = _dot(hb, w_ref[:, (2 + j) * d:(3 + j) * d]).astype(BF16)
        gate_ref[:, j * d:(j + 1) * d] = _silu(
            _dot(hb, w_ref[:, (4 + j) * d:(5 + j) * d])).astype(BF16)


def _ret_proj(x2d, g, w_in, cos, sin, seq):
    t, d = x2d.shape
    tm = min(TOKEN_TILE, seq)
    n_seq = seq // tm
    tok = lambda i: (i, 0)
    pos = lambda i: (i % n_seq, 0)
    return pl.pallas_call(
        _ret_proj_kernel,
        grid=(t // tm,),
        in_specs=[
            pl.BlockSpec((tm, d), tok),
            _const_spec((1, d)),
            _const_spec((d, 6 * d)),
            pl.BlockSpec((tm, LANES), pos),
            pl.BlockSpec((tm, LANES), pos),
        ],
        out_specs=[
            pl.BlockSpec((tm, d), tok),
            pl.BlockSpec((tm, d), tok),
            pl.BlockSpec((tm, 2 * d), tok),
            pl.BlockSpec((tm, 2 * d), tok),
        ],
        out_shape=[
            jax.ShapeDtypeStruct((t, d), BF16),
            jax.ShapeDtypeStruct((t, d), BF16),
            jax.ShapeDtypeStruct((t, 2 * d), BF16),
            jax.ShapeDtypeStruct((t, 2 * d), BF16),
        ],
        compiler_params=_params(("parallel",)),
        name="ret_proj",
    )(x2d, g, w_in, cos, sin)


def _retention_kernel(cd_ref, q_ref, k_ref, v_ref, gate_ref, x_ref, wout_ref,
                      intra_ref, qdec_ref, kdec_ref, o_ref, state_ref, gated_ref):
    c = RET_CHUNK
    n_chunks = q_ref.shape[0] // c

    @pl.when(pl.program_id(1) == 0)
    def _():
        state_ref[...] = jnp.zeros_like(state_ref)

    for h in range(RET_HEADS):
        qk = slice(h * RET_QK_DIM, (h + 1) * RET_QK_DIM)
        vv = slice(h * RET_V_DIM, (h + 1) * RET_V_DIM)
        for n in range(n_chunks):
            rows = slice(n * c, (n + 1) * c)
            qi = q_ref[rows, qk]
            ki = k_ref[rows, qk]
            vi = v_ref[rows, vv]
            scores = _dot_nt(qi, ki) * intra_ref[h]
            intra = _dot(scores.astype(BF16), vi)
            state = state_ref[h]
            inter = _dot(qi, state.astype(BF16)) * qdec_ref[h]
            kd = (ki.astype(F32) * kdec_ref[h]).astype(BF16)
            state_ref[h] = state * cd_ref[h] + _dot_tn(kd, vi)
            o = _rms(intra + inter, NORM_EPS)
            gated_ref[rows, vv] = (gate_ref[rows, vv].astype(F32) * o).astype(BF16)
    o_ref[...] = x_ref[...] + _dot(gated_ref[...], wout_ref[...])


def _retention(chunk_decay, q, k, v, gate, x, w_out, intra_decay, q_decay, k_decay):
    b, s, d = x.shape
    ts = min(TOKEN_TILE, s)
    blk = lambda w: pl.BlockSpec((None, ts, w), lambda bi, si: (bi, si, 0))
    return pl.pallas_call(
        _retention_kernel,
        grid=(b, s // ts),
        in_specs=[
            pl.BlockSpec(memory_space=pltpu.SMEM),
            blk(d), blk(d), blk(2 * d), blk(2 * d), blk(d),
            _const_spec((2 * d, d)),
            _const_spec((RET_HEADS, RET_CHUNK, RET_CHUNK)),
            _const_spec((RET_HEADS, RET_CHUNK, 1)),
            _const_spec((RET_HEADS, RET_CHUNK, 1)),
        ],
        out_specs=blk(d),
        out_shape=jax.ShapeDtypeStruct((b, s, d), F32),
        scratch_shapes=[
            pltpu.VMEM((RET_HEADS, RET_QK_DIM, RET_V_DIM), F32),
            pltpu.VMEM((ts, 2 * d), BF16),
        ],
        compiler_params=_params(("parallel", "arbitrary")),
        name="retention",
    )(chunk_decay, q, k, v, gate, x, w_out, intra_decay, q_decay, k_decay)


def _ffn_kernel(x_ref, g_ref, wg_ref, wu_ref, wd_ref, o_ref, *, f_chunks):
    x = x_ref[...]
    hb = (_rms(x, NORM_EPS) * g_ref[...]).astype(BF16)
    acc = x
    for lo, hi in f_chunks:
        a = _dot(hb, wg_ref[:, lo:hi])
        u = _dot(hb, wu_ref[:, lo:hi])
        acc = acc + _dot((_silu(a) * u).astype(BF16), wd_ref[lo:hi, :])
    o_ref[...] = acc


def _chunks(total, size):
    return tuple((lo, min(lo + size, total)) for lo in range(0, total, size))


def _ffn(x2d, g, wg, wu, wd):
    t, d = x2d.shape
    f = wg.shape[1]
    tm = min(TOKEN_TILE, t)
    tok = lambda i: (i, 0)
    return pl.pallas_call(
        functools.partial(_ffn_kernel, f_chunks=_chunks(f, 1024)),
        grid=(t // tm,),
        in_specs=[
            pl.BlockSpec((tm, d), tok),
            _const_spec((1, d)),
            _const_spec((d, f)),
            _const_spec((d, f)),
            _const_spec((f, d)),
        ],
        out_specs=pl.BlockSpec((tm, d), tok),
        out_shape=jax.ShapeDtypeStruct((t, d), F32),
        compiler_params=_params(("parallel",)),
        name="dense_ffn",
    )(x2d, g, wg, wu, wd)


def _rope64(y, cos, sin_a, sin_b):
    half = DIFF_HEAD_DIM // 2
    return (y * cos + pltpu.roll(y, LANES - half, 1) * sin_a
            + pltpu.roll(y, half, 1) * sin_b)


def _kvq_proj_kernel(x_ref, gkv_ref, gq_ref, wk_ref, wvt_ref, wqt_ref, cos_ref, sa_ref, sb_ref,
                     cost_ref, sint_ref, k_ref, vt_ref, qt_ref):
    d = D_MODEL
    xn = _rms(x_ref[...], NORM_EPS)
    hkv = (xn * gkv_ref[...]).astype(BF16)
    hq = (xn * gq_ref[...]).astype(BF16)
    cos = cos_ref[...]
    sa = sa_ref[...]
    sb = sb_ref[...]
    yk = _dot(hkv, wk_ref[...])
    for j in range(d // LANES):
        cols = slice(j * LANES, (j + 1) * LANES)
        k_ref[:, cols] = _rope64(yk[:, cols], cos, sa, sb).astype(BF16)

    yvt = _dot_nt(wvt_ref[...], hkv)
    for h in range(DIFF_HEADS):
        vt_ref[h] = yvt[h * DIFF_V_DIM:(h + 1) * DIFF_V_DIM, :].astype(BF16)

    yqt = _dot_nt(wqt_ref[...], hq)
    ct = cost_ref[...]
    st = sint_ref[...]
    half = DIFF_HEAD_DIM // 2
    q_scale = DIFF_HEAD_DIM ** -0.5 * LOG2E
    for g in range(d // DIFF_HEAD_DIM):
        r0 = g * DIFF_HEAD_DIM
        a = yqt[r0:r0 + half, :]
        b = yqt[r0 + half:r0 + 2 * half, :]
        qt_ref[r0:r0 + half, :] = ((a * ct - b * st) * q_scale).astype(BF16)
        qt_ref[r0 + half:r0 + 2 * half, :] = ((b * ct + a * st) * q_scale).astype(BF16)


def _kvq_proj(x, gkv, gq, wk, wvt, wqt, cos, sin_a, sin_b, cos_t, sin_t):
    b, s, d = x.shape
    tm = min(TOKEN_TILE, s)
    n_seq = s // tm
    tk = min(ATTN_TILE, s)
    per_blk = tk // tm
    half = DIFF_HEAD_DIM // 2
    pos = lambda bi, i: (i, 0)
    pos_t = lambda bi, i: (0, i)
    return pl.pallas_call(
        _kvq_proj_kernel,
        grid=(b, n_seq),
        in_specs=[
            pl.BlockSpec((None, tm, d), lambda bi, i: (bi, i, 0)),
            _const_spec((1, d)),
            _const_spec((1, d)),
            _const_spec((d, d)),
            _const_spec((d, d)),
            _const_spec((d, d)),
            pl.BlockSpec((tm, LANES), pos),
            pl.BlockSpec((tm, LANES), pos),
            pl.BlockSpec((tm, LANES), pos),
            pl.BlockSpec((half, tm), pos_t),
            pl.BlockSpec((half, tm), pos_t),
        ],
        out_specs=[
            pl.BlockSpec((None, tm, d), lambda bi, i: (bi, i, 0)),
            pl.BlockSpec((None, DIFF_HEADS, None, DIFF_V_DIM, tm),
                         lambda bi, i: (bi, 0, i // per_blk, 0, i % per_blk)),
            pl.BlockSpec((None, d, tm), lambda bi, i: (bi, 0, i)),
        ],
        out_shape=[
            jax.ShapeDtypeStruct((b, s, d), BF16),
            jax.ShapeDtypeStruct((b, DIFF_HEADS, s // tk, DIFF_V_DIM, tk), BF16),
            jax.ShapeDtypeStruct((b, d, s), BF16),
        ],
        compiler_params=_params(("parallel", "parallel")),
        name="kvq_proj",
    )(x, gkv, gq, wk, wvt, wqt, cos, sin_a, sin_b, cos_t, sin_t)


MXU_COLS = 512
MASKED_LOGIT = -1e30
LOGITS_AHEAD = 1


def _attn_kernel(lam_ref, qt_ref, k_ref, vt_ref, g_ref, o_ref, qq_ref, m_ref, l_ref, acc_ref,
                 s_ref, *, out_scale):
    tq = qt_ref.shape[1]
    tk = vt_ref.shape[2]
    assert tq == tk
    r = 2 * tq
    qi = pl.program_id(2)

    qt = qt_ref[...]
    feat = lax.broadcasted_iota(jnp.int32, qt.shape, 0)
    zero = jnp.zeros_like(qt)
    qq_ref[:, 0:tq] = jnp.where(feat < DIFF_HEAD_DIM, qt, zero)
    qq_ref[:, tq:r] = jnp.where(feat >= DIFF_HEAD_DIM, qt, zero)
    m_ref[...] = jnp.full_like(m_ref, MASKED_LOGIT)
    l_ref[...] = jnp.zeros_like(l_ref)
    acc_ref[...] = jnp.zeros_like(acc_ref)

    n_groups = r // MXU_COLS
    ahead = s_ref.shape[0]
    assert n_groups >= ahead

    def keys_needed(c):
        return min(tk, (c * MXU_COLS) % tq + MXU_COLS)

    def logits(j, c, n_keys=tk):
        start = pl.multiple_of(j * tk, tk)
        return _dot(k_ref[pl.ds(start, n_keys), :],
                    qq_ref[:, c * MXU_COLS:(c + 1) * MXU_COLS])

    def block(j, is_diagonal):
        vt = vt_ref[j]
        pending = [s_ref[a] for a in range(ahead)]
        for c in range(n_groups):
            cols = slice(c * MXU_COLS, (c + 1) * MXU_COLS)
            s = pending.pop(0)
            if c + ahead < n_groups:
                n_next = keys_needed(c + ahead) if is_diagonal else tk
                pending.append(logits(j, c + ahead, n_next))
            elif not is_diagonal:
                s_ref[c + ahead - n_groups] = logits(j + 1, c + ahead - n_groups)
            vt_c = vt
            if is_diagonal:
                n_keys = keys_needed(c)
                s = s[0:n_keys, :]
                vt_c = vt[:, 0:n_keys]
                key = lax.broadcasted_iota(jnp.int32, s.shape, 0)
                qry = lax.broadcasted_iota(jnp.int32, s.shape, 1) + (c * MXU_COLS) % tq
                s = jnp.where(key <= qry, s, MASKED_LOGIT)
            m_prev = m_ref[:, cols]
            m_new = jnp.maximum(m_prev, jnp.max(s, axis=0, keepdims=True))
            alpha = jnp.exp2(m_prev - m_new)
            p = jnp.exp2(s - m_new)
            l_ref[:, cols] = alpha * l_ref[:, cols] + jnp.sum(p, axis=0, keepdims=True)
            acc_ref[:, cols] = alpha * acc_ref[:, cols] + _dot(vt_c, p.astype(BF16))
            m_ref[:, cols] = m_new

    def body(j, carry):
        block(j, False)
        return carry

    for a in range(ahead):
        s_ref[a] = logits(0, a)
    lax.fori_loop(0, qi, body, 0)
    block(qi, True)

    o = acc_ref[...] * (1.0 / l_ref[...])
    out = o[:, 0:tq] - lam_ref[0] * o[:, tq:r]
    ms = jnp.mean(out * out, axis=0, keepdims=True)
    out = out * lax.rsqrt(ms + SUBLN_EPS) * g_ref[...] * out_scale
    o_ref[...] = out.T.astype(BF16)


def _diff_attention(lam, qt, k, vt, subln_g_cols, lambda_init):
    b, s, d = k.shape
    tq = vt.shape[-1]
    n_blk = s // tq
    return pl.pallas_call(
        functools.partial(_attn_kernel, out_scale=1.0 - lambda_init),
        grid=(b, DIFF_HEADS, n_blk),
        in_specs=[
            pl.BlockSpec(memory_space=pltpu.SMEM),
            pl.BlockSpec((None, LANES, tq), lambda bi, h, i: (bi, h, i)),
            pl.BlockSpec((None, s, LANES), lambda bi, h, i: (bi, 0, h)),
            pl.BlockSpec((None, None, n_blk, DIFF_V_DIM, tq), lambda bi, h, i: (bi, h, 0, 0, 0)),
            _const_spec((DIFF_V_DIM, tq)),
        ],
        out_specs=pl.BlockSpec((None, tq, LANES), lambda bi, h, i: (bi, i, h)),
        out_shape=jax.ShapeDtypeStruct((b, s, d), BF16),
        scratch_shapes=[
            pltpu.VMEM((LANES, 2 * tq), BF16),
            pltpu.VMEM((1, 2 * tq), F32),
            pltpu.VMEM((1, 2 * tq), F32),
            pltpu.VMEM((DIFF_V_DIM, 2 * tq), F32),
            pltpu.VMEM((LOGITS_AHEAD, tq, MXU_COLS), F32),
        ],
        compiler_params=_params(("parallel", "parallel", "arbitrary")),
        name="diff_attention",
    )(lam, qt, k, vt, subln_g_cols)


def _split_bf16(x):
    hi = x.astype(BF16)
    lo = (x - hi.astype(F32)).astype(BF16)
    return hi, lo


ROUTE_ROWS = 8


def _oproj_router_kernel(a_ref, x_ref, wo_ref, g_ref, wrt_ref, x_out_ref, h_ref, route_ref):
    x = x_ref[...] + _dot(a_ref[...], wo_ref[...])
    x_out_ref[...] = x
    h = _rms(x, NORM_EPS) * g_ref[...]
    h_hi, h_lo = _split_bf16(h)
    h_ref[...] = h_hi
    wrt = wrt_ref[...]
    a = _dot_nt(wrt, h_hi)
    b = _dot_nt(wrt, h_lo)
    ne = N_EXPERTS
    logits = a[0:ne, :] + (b[0:ne, :] + a[ne:2 * ne, :])
    expert = lax.broadcasted_iota(jnp.int32, logits.shape, 0).astype(F32)
    neg = -jnp.inf
    m1 = jnp.max(logits, axis=0, keepdims=True)
    i1 = jnp.min(jnp.where(logits == m1, expert, float(ne)), axis=0, keepdims=True)
    rest = jnp.where(expert == i1, neg, logits)
    m2 = jnp.max(rest, axis=0, keepdims=True)
    i2 = jnp.min(jnp.where(rest == m2, expert, float(ne)), axis=0, keepdims=True)
    e = jnp.exp(m2 - m1)
    w1 = 1.0 / (1.0 + e)
    w2 = e / (1.0 + e)
    rec = lax.broadcasted_iota(jnp.int32, (ROUTE_ROWS, logits.shape[1]), 0)
    route_ref[...] = jnp.where(rec == 0, i1, jnp.where(rec == 1, i2,
                               jnp.where(rec == 2, w1, jnp.where(rec == 3, w2, 0.0))))


def _oproj_router(attn2d, x2d, wo, g, wrt):
    t, d = x2d.shape
    tm = min(TOKEN_TILE, t)
    tok = lambda i: (i, 0)
    return pl.pallas_call(
        _oproj_router_kernel,
        grid=(t // tm,),
        in_specs=[
            pl.BlockSpec((tm, d), tok),
            pl.BlockSpec((tm, d), tok),
            _const_spec((d, d)),
            _const_spec((1, d)),
            _const_spec((2 * N_EXPERTS, d)),
        ],
        out_specs=[
            pl.BlockSpec((tm, d), tok),
            pl.BlockSpec((tm, d), tok),
            pl.BlockSpec((ROUTE_ROWS, tm), lambda i: (0, i)),
        ],
        out_shape=[
            jax.ShapeDtypeStruct((t, d), F32),
            jax.ShapeDtypeStruct((t, d), BF16),
            jax.ShapeDtypeStruct((ROUTE_ROWS, t), F32),
        ],
        compiler_params=_params(("parallel",)),
        name="oproj_router",
    )(attn2d, x2d, wo, g, wrt)


WEIGHT_SLOTS = 2


def _expert_kernel(te_ref, nt_ref, xs_ref, rw_ref, wg_hbm, wu_hbm, wd_hbm, y_prev_hbm, y_ref,
                   wg_ref, wu_ref, wd_ref, sg_ref, su_ref, sd_ref, sem, *, fc):
    del y_prev_hbm
    i = pl.program_id(0)
    e = te_ref[i]
    active = i < nt_ref[0]
    first = jnp.logical_and(active, jnp.logical_or(i == 0, e != te_ref[jnp.maximum(i - 1, 0)]))
    n_chunks = wg_ref.shape[1] // fc

    def chunk_copies(c):
        slot = c % WEIGHT_SLOTS
        cols = pl.ds(c * fc, fc)
        return (
            pltpu.make_async_copy(wg_hbm.at[e, :, cols], sg_ref.at[slot], sem.at[0, slot]),
            pltpu.make_async_copy(wu_hbm.at[e, :, cols], su_ref.at[slot], sem.at[1, slot]),
            pltpu.make_async_copy(wd_hbm.at[e, cols, :], sd_ref.at[slot], sem.at[2, slot]),
        )

    def swiglu_chunk(xs, c):
        lo, hi = c * fc, (c + 1) * fc
        a = _dot(xs, wg_ref[:, lo:hi])
        u = _dot(xs, wu_ref[:, lo:hi])
        return _dot((_silu(a) * u).astype(BF16), wd_ref[lo:hi, :])

    def finish(acc):
        y_ref[...] = (acc * rw_ref[...]).astype(y_ref.dtype)

    @pl.when(first)
    def _():
        for c in range(min(WEIGHT_SLOTS, n_chunks)):
            for cp in chunk_copies(c):
                cp.start()
        xs = xs_ref[...]
        acc = jnp.zeros(y_ref.shape, F32)
        for c in range(n_chunks):
            slot = c % WEIGHT_SLOTS
            lo, hi = c * fc, (c + 1) * fc
            for cp in chunk_copies(c):
                cp.wait()
            wg_ref[:, lo:hi] = sg_ref[slot].astype(BF16)
            wu_ref[:, lo:hi] = su_ref[slot].astype(BF16)
            wd_ref[lo:hi, :] = sd_ref[slot].astype(BF16)
            if c + WEIGHT_SLOTS < n_chunks:
                for cp in chunk_copies(c + WEIGHT_SLOTS):
                    cp.start()
            acc = acc + swiglu_chunk(xs, c)
        finish(acc)

    @pl.when(jnp.logical_and(active, jnp.logical_not(first)))
    def _():
        xs = xs_ref[...]
        acc = jnp.zeros(y_ref.shape, F32)
        for c in range(n_chunks):
            acc = acc + swiglu_chunk(xs, c)
        finish(acc)

    @pl.when(jnp.logical_not(active))
    def _():
        y_ref[...] = jnp.zeros_like(y_ref)


def _experts(tile_expert, n_tiles_used, xs, row_w, wg, wu, wd, y_prev, *, tile_lo):
    p, d = xs.shape
    f = wg.shape[2]
    tm = EXPERT_TILE
    fc = EXPERT_F_CHUNK
    assert f % fc == 0
    n_tiles = p // tm

    def row(i, te, nt):
        return (jnp.maximum(jnp.minimum(i, nt[0] - 1), 0), 0)

    operands = (tile_expert, n_tiles_used, xs, row_w, wg, wu, wd, y_prev)
    grid_spec = pltpu.PrefetchScalarGridSpec(
        num_scalar_prefetch=2,
        grid=(n_tiles,),
        in_specs=[
            pl.BlockSpec((tm, d), row),
            pl.BlockSpec((tm, 1), row),
            pl.BlockSpec(memory_space=pl.ANY),
            pl.BlockSpec(memory_space=pl.ANY),
            pl.BlockSpec(memory_space=pl.ANY),
            pl.BlockSpec(memory_space=pl.ANY),
        ],
        out_specs=pl.BlockSpec((tm, d), lambda i, te, nt: (i + tile_lo, 0)),
        scratch_shapes=[
            pltpu.VMEM((d, f), BF16),
            pltpu.VMEM((d, f), BF16),
            pltpu.VMEM((f, d), BF16),
            pltpu.VMEM((WEIGHT_SLOTS, d, fc), F32),
            pltpu.VMEM((WEIGHT_SLOTS, d, fc), F32),
            pltpu.VMEM((WEIGHT_SLOTS, fc, d), F32),
            pltpu.SemaphoreType.DMA((3, WEIGHT_SLOTS)),
        ],
    )
    return pl.pallas_call(
        functools.partial(_expert_kernel, fc=fc),
        grid_spec=grid_spec,
        out_shape=jax.ShapeDtypeStruct(y_prev.shape, y_prev.dtype),
        input_output_aliases={len(operands) - 1: 0},
        compiler_params=_params(("arbitrary",)),
        name="experts",
    )(*operands)


def _route_tables(route, tm):
    t = route.shape[1]
    n_pairs = t * TOP_K
    n_tiles = n_pairs // tm + N_EXPERTS - 1
    p = n_tiles * tm
    e_pair = route[0:TOP_K, :].astype(jnp.int32).reshape(n_pairs)
    w_pair = route[TOP_K:2 * TOP_K, :].reshape(n_pairs)
    onehot = (e_pair[None, :] == jnp.arange(N_EXPERTS, dtype=jnp.int32)[:, None]).astype(jnp.int32)
    counts = jnp.sum(onehot, axis=1)
    rank = jnp.sum((jnp.cumsum(onehot, axis=1) - onehot) * onehot, axis=0)
    tiles_per = (counts + tm - 1) // tm
    tile_end = jnp.cumsum(tiles_per)
    tile_start = tile_end - tiles_per
    dest = tile_start[e_pair] * tm + rank
    tile_ids = jnp.arange(n_tiles, dtype=jnp.int32)
    tile_expert = jnp.minimum(
        jnp.sum((tile_ids[:, None] >= tile_end[None, :]).astype(jnp.int32), axis=1),
        N_EXPERTS - 1)
    n_used = tile_end[-1:].astype(jnp.int32)
    pair_ids = jnp.arange(n_pairs, dtype=jnp.int32)
    by_expert = jnp.sort(e_pair * n_pairs + pair_ids) % n_pairs
    group_first = jnp.cumsum(counts) - counts
    row_in_group = tile_ids * tm - tile_start[tile_expert] * tm
    slot = (group_first[tile_expert] + row_in_group)[:, None] + jnp.arange(tm, dtype=jnp.int32)[None, :]
    valid = (row_in_group[:, None] + jnp.arange(tm, dtype=jnp.int32)[None, :]
             < counts[tile_expert][:, None])
    row_pair = jnp.where(valid, by_expert[jnp.clip(slot, 0, n_pairs - 1)], n_pairs).reshape(p)
    row_token = jnp.where(row_pair < n_pairs, row_pair % t, 0)
    row_w = jnp.concatenate([w_pair, jnp.zeros((1,), F32)])[row_pair]
    return row_token, row_w.reshape(p, 1), tile_expert, n_used, dest


def _final_kernel(x_ref, ya_ref, yb_ref, g_ref, o_ref):
    x = x_ref[...] + (ya_ref[...].astype(F32) + yb_ref[...].astype(F32))
    o_ref[...] = _rms(x, NORM_EPS) * g_ref[...]


def _final(x2d, y_picks, g, *, tile_lo):
    t, d = x2d.shape
    tm = min(2 * TOKEN_TILE, t)
    n_steps = y_picks.shape[0] // (TOP_K * tm)
    tok = lambda i: (i + tile_lo, 0)
    return pl.pallas_call(
        _final_kernel,
        grid=(n_steps,),
        in_specs=[
            pl.BlockSpec((tm, d), tok),
            pl.BlockSpec((tm, d), lambda i: (i, 0)),
            pl.BlockSpec((tm, d), lambda i: (i + n_steps, 0)),
            _const_spec((1, d)),
        ],
        out_specs=pl.BlockSpec((tm, d), tok),
        out_shape=jax.ShapeDtypeStruct((t, d), F32),
        input_output_aliases={0: 0},
        compiler_params=_params(("parallel",)),
        name="final_norm",
    )(x2d, y_picks, y_picks, g)


def _rope_angles(seq, dim):
    inv = 1.0 / (ROPE_THETA ** (jnp.arange(0, dim, 2, dtype=F32) / dim))
    return jnp.arange(seq, dtype=F32)[:, None] * inv[None, :]


def _retention_decays():
    h, c = RET_HEADS, RET_CHUNK
    log_gamma = jnp.log1p(-jnp.exp2(-5.0 - jnp.arange(h, dtype=F32)))
    pos = jnp.arange(c, dtype=F32)
    rel = pos[:, None] - pos[None, :]
    intra = jnp.where(rel >= 0, jnp.exp(log_gamma[:, None, None] * jnp.maximum(rel, 0.0)), 0.0)
    q_decay = jnp.exp(log_gamma[:, None] * (pos + 1.0))[:, :, None]
    k_decay = jnp.exp(log_gamma[:, None] * (c - 1.0 - pos))[:, :, None]
    chunk_decay = jnp.exp(log_gamma * c)
    return chunk_decay, intra, q_decay, k_decay


def kernel(x, attn_norm_g, ffn_norm_g, ret_w_in, ret_w_out, kv_norm_g, w_kv, diff_w_q,
           diff_lam_q1, diff_lam_k1, diff_lam_q2, diff_lam_k2, diff_subln_g, diff_w_o,
           ffn_w_gate, ffn_w_up, ffn_w_down, moe_w_router, moe_w_gate, moe_w_up, moe_w_down,
           final_norm_g):
    b, s, d = x.shape
    t = b * s
    row = lambda g: g.reshape(1, -1).astype(F32)
    bf = lambda w: w.astype(BF16)

    ang = _rope_angles(s, RET_QK_DIM)
    q, k, v, gate = _ret_proj(x.reshape(t, d), row(attn_norm_g[0]), bf(ret_w_in[0]),
                              jnp.cos(ang), jnp.sin(ang), s)
    chunk_decay, intra_decay, q_decay, k_decay = _retention_decays()
    x1 = _retention(chunk_decay, q.reshape(b, s, d), k.reshape(b, s, d),
                    v.reshape(b, s, 2 * d), gate.reshape(b, s, 2 * d), x,
                    bf(ret_w_out[0]), intra_decay, q_decay, k_decay)
    x2 = _ffn(x1.reshape(t, d), row(ffn_norm_g[0]), bf(ffn_w_gate[0]), bf(ffn_w_up[0]),
              bf(ffn_w_down[0]))

    ang = _rope_angles(s, DIFF_HEAD_DIM)
    cos32, sin32 = lax.optimization_barrier((jnp.cos(ang), jnp.sin(ang)))
    zero32 = jnp.zeros_like(sin32)
    pairs = LANES // DIFF_HEAD_DIM
    cos = jnp.concatenate([cos32, cos32] * pairs, axis=-1)
    sin_a = jnp.concatenate([-sin32, zero32] * pairs, axis=-1)
    sin_b = jnp.concatenate([zero32, sin32] * pairs, axis=-1)
    ks, vt, qt = _kvq_proj(x2.reshape(b, s, d), row(kv_norm_g), row(attn_norm_g[1]),
                           bf(w_kv[:, :d]), bf(w_kv[:, d:].T), bf(diff_w_q[0].T),
                           cos, sin_a, sin_b, cos32.T, sin32.T)
    lambda_init = 0.8 - 0.6 * math.exp(-0.3 * 1)
    lam = (jnp.exp(jnp.sum(diff_lam_q1[0].astype(F32) * diff_lam_k1[0].astype(F32)))
           - jnp.exp(jnp.sum(diff_lam_q2[0].astype(F32) * diff_lam_k2[0].astype(F32)))
           + lambda_init).reshape(1)
    g_cols = jnp.broadcast_to(diff_subln_g[0].astype(F32)[:, None], (DIFF_V_DIM, vt.shape[-1]))
    attn = _diff_attention(lam, qt, ks, vt, g_cols, lambda_init)

    wr_t = moe_w_router[0].astype(F32).T
    wr_hi = wr_t.astype(BF16)
    wr_lo = (wr_t - wr_hi.astype(F32)).astype(BF16)
    x3, h, route = _oproj_router(attn.reshape(t, d), x2, bf(diff_w_o[0]), row(ffn_norm_g[1]),
                                 jnp.concatenate([wr_hi, wr_lo], axis=0))
    row_token, row_w, tile_expert, n_used, dest = _route_tables(route, EXPERT_TILE)
    tm = EXPERT_TILE
    n_tiles = tile_expert.shape[0]
    n_head = n_tiles // 4
    y = jnp.zeros((n_tiles * tm, d), BF16)
    for lo, hi in ((0, n_head), (n_head, n_tiles)):
        xs = jnp.take(h, row_token[lo * tm:hi * tm], axis=0, mode="clip")
        y = _experts(tile_expert[lo:hi], jnp.clip(n_used - lo, 0, hi - lo), xs,
                     row_w[lo * tm:hi * tm], moe_w_gate[0], moe_w_up[0], moe_w_down[0], y,
                     tile_lo=lo)
    dest = dest.reshape(TOP_K, t)
    out = x3
    half = t // 2
    for lo in (0, half):
        picks = jnp.take(y, dest[:, lo:lo + half].reshape(TOP_K * half), axis=0, mode="clip")
        out = _final(out, picks, row(final_norm_g), tile_lo=lo // min(2 * TOKEN_TILE, t))
    return out.reshape(b, s, d)
```

```python
import functools
import math

import jax
import jax.numpy as jnp
from jax import lax
from jax.experimental import pallas as pl
from jax.experimental.pallas import tpu as pltpu

F32 = jnp.float32
BF16 = jnp.bfloat16

D_MODEL = 1024
RET_HEADS = 4
RET_QK_DIM = D_MODEL // RET_HEADS
RET_V_DIM = 2 * D_MODEL // RET_HEADS
RET_CHUNK = 256
DIFF_HEADS = 8
DIFF_HEAD_DIM = D_MODEL // DIFF_HEADS // 2
DIFF_V_DIM = 2 * DIFF_HEAD_DIM
BF16_SUBLANES = 16
V_ROWS = DIFF_V_DIM + BF16_SUBLANES
N_EXPERTS = 8
TOP_K = 2
ROPE_THETA = 10000.0
NORM_EPS = 1e-6
SUBLN_EPS = 1e-5
LOG2E = math.log2(math.e)

LANES = 128
VMEM_LIMIT = 56 * 1024 * 1024

TOKEN_TILE = 512
ATTN_TILE = 2048
EXPERT_TILE = 512
EXPERT_F_CHUNK = 512


def _dot(a, b):
    return jnp.dot(a, b, preferred_element_type=F32)


def _dot_nt(a, b):
    return lax.dot_general(a, b, (((1,), (1,)), ((), ())), preferred_element_type=F32)


def _dot_tn(a, b):
    return lax.dot_general(a, b, (((0,), (0,)), ((), ())), preferred_element_type=F32)


def _rms(x, eps):
    return x * lax.rsqrt(jnp.mean(x * x, axis=-1, keepdims=True) + eps)


def _silu(x):
    return x * (1.0 / (1.0 + jnp.exp(-x)))


def _const_spec(shape):
    zeros = (0,) * len(shape)
    return pl.BlockSpec(shape, lambda *_: zeros, pipeline_mode=pl.Buffered(1))


def _params(semantics):
    return pltpu.CompilerParams(dimension_semantics=semantics, vmem_limit_bytes=VMEM_LIMIT)


def _ret_proj_kernel(x_ref, g_ref, w_ref, cos_ref, sin_ref, q_ref, k_ref, v_ref, gate_ref):
    d = D_MODEL
    hb = (_rms(x_ref[...], NORM_EPS) * g_ref[...]).astype(BF16)
    c = cos_ref[...]
    s = sin_ref[...]
    half = RET_QK_DIM // 2
    for dst, col0, scale in ((q_ref, 0, 1.0), (k_ref, d, RET_QK_DIM ** -0.5)):
        y = _dot(hb, w_ref[:, col0:col0 + d])
        for h in range(RET_HEADS):
            lo = h * RET_QK_DIM
            y1 = y[:, lo:lo + half]
            y2 = y[:, lo + half:lo + 2 * half]
            dst[:, lo:lo + half] = ((y1 * c - y2 * s) * scale).astype(BF16)
            dst[:, lo + half:lo + 2 * half] = ((y2 * c + y1 * s) * scale).astype(BF16)
    for j in range(2):
        v_ref[:, j * d:(j + 1) * d] = _dot(hb, w_ref[:, (2 + j) * d:(3 + j) * d]).astype(BF16)
        gate_ref[:, j * d:(j + 1) * d] = _silu(
            _dot(hb, w_ref[:, (4 + j) * d:(5 + j) * d])).astype(BF16)


def _ret_proj(x2d, g, w_in, cos, sin, seq):
    t, d = x2d.shape
    tm = min(TOKEN_TILE, seq)
    n_seq = seq // tm
    tok = lambda i: (i, 0)
    pos = lambda i: (i % n_seq, 0)
    return pl.pallas_call(
        _ret_proj_kernel,
        grid=(t // tm,),
        in_specs=[
            pl.BlockSpec((tm, d), tok),
            _const_spec((1, d)),
            _const_spec((d, 6 * d)),
            pl.BlockSpec((tm, LANES), pos),
            pl.BlockSpec((tm, LANES), pos),
        ],
        out_specs=[
            pl.BlockSpec((tm, d), tok),
            pl.BlockSpec((tm, d), tok),
            pl.BlockSpec((tm, 2 * d), tok),
            pl.BlockSpec((tm, 2 * d), tok),
        ],
        out_shape=[
            jax.ShapeDtypeStruct((t, d), BF16),
            jax.ShapeDtypeStruct((t, d), BF16),
            jax.ShapeDtypeStruct((t, 2 * d), BF16),
            jax.ShapeDtypeStruct((t, 2 * d), BF16),
        ],
        compiler_params=_params(("parallel",)),
        name="ret_proj",
    )(x2d, g, w_in, cos, sin)


def _retention_kernel(cd_ref, q_ref, k_ref, v_ref, gate_ref, x_ref, wout_ref,
                      intra_ref, qdec_ref, kdec_ref, o_ref, state_ref, gated_ref):
    c = RET_CHUNK
    n_chunks = q_ref.shape[0] // c

    @pl.when(pl.program_id(1) == 0)
    def _():
        state_ref[...] = jnp.zeros_like(state_ref)

    for h in range(RET_HEADS):
        qk = slice(h * RET_QK_DIM, (h + 1) * RET_QK_DIM)
        vv = slice(h * RET_V_DIM, (h + 1) * RET_V_DIM)
        for n in range(n_chunks):
            rows = slice(n * c, (n + 1) * c)
            qi = q_ref[rows, qk]
            ki = k_ref[rows, qk]
            vi = v_ref[rows, vv]
            scores = _dot_nt(qi, ki) * intra_ref[h]
            intra = _dot(scores.astype(BF16), vi)
            state = state_ref[h]
            inter = _dot(qi, state.astype(BF16)) * qdec_ref[h]
            kd = (ki.astype(F32) * kdec_ref[h]).astype(BF16)
            state_ref[h] = state * cd_ref[h] + _dot_tn(kd, vi)
            o = _rms(intra + inter, NORM_EPS)
            gated_ref[rows, vv] = (gate_ref[rows, vv].astype(F32) * o).astype(BF16)
    o_ref[...] = x_ref[...] + _dot(gated_ref[...], wout_ref[...])


def _retention(chunk_decay, q, k, v, gate, x, w_out, intra_decay, q_decay, k_decay):
    b, s, d = x.shape
    ts = min(TOKEN_TILE, s)
    blk = lambda w: pl.BlockSpec((None, ts, w), lambda bi, si: (bi, si, 0))
    return pl.pallas_call(
        _retention_kernel,
        grid=(b, s // ts),
        in_specs=[
            pl.BlockSpec(memory_space=pltpu.SMEM),
            blk(d), blk(d), blk(2 * d), blk(2 * d), blk(d),
            _const_spec((2 * d, d)),
            _const_spec((RET_HEADS, RET_CHUNK, RET_CHUNK)),
            _const_spec((RET_HEADS, RET_CHUNK, 1)),
            _const_spec((RET_HEADS, RET_CHUNK, 1)),
        ],
        out_specs=blk(d),
        out_shape=jax.ShapeDtypeStruct((b, s, d), F32),
        scratch_shapes=[
            pltpu.VMEM((RET_HEADS, RET_QK_DIM, RET_V_DIM), F32),
            pltpu.VMEM((ts, 2 * d), BF16),
        ],
        compiler_params=_params(("parallel", "arbitrary")),
        name="retention",
    )(chunk_decay, q, k, v, gate, x, w_out, intra_decay, q_decay, k_decay)


def _ffn_kernel(x_ref, g_ref, wg_ref, wu_ref, wd_ref, o_ref, *, f_chunks):
    x = x_ref[...]
    hb = (_rms(x, NORM_EPS) * g_ref[...]).astype(BF16)
    acc = x
    for lo, hi in f_chunks:
        a = _dot(hb, wg_ref[:, lo:hi])
        u = _dot(hb, wu_ref[:, lo:hi])
        acc = acc + _dot((_silu(a) * u).astype(BF16), wd_ref[lo:hi, :])
    o_ref[...] = acc


def _chunks(total, size):
    return tuple((lo, min(lo + size, total)) for lo in range(0, total, size))


def _ffn(x2d, g, wg, wu, wd):
    t, d = x2d.shape
    f = wg.shape[1]
    tm = min(TOKEN_TILE, t)
    tok = lambda i: (i, 0)
    return pl.pallas_call(
        functools.partial(_ffn_kernel, f_chunks=_chunks(f, 1024)),
        grid=(t // tm,),
        in_specs=[
            pl.BlockSpec((tm, d), tok),
            _const_spec((1, d)),
            _const_spec((d, f)),
            _const_spec((d, f)),
            _const_spec((f, d)),
        ],
        out_specs=pl.BlockSpec((tm, d), tok),
        out_shape=jax.ShapeDtypeStruct((t, d), F32),
        compiler_params=_params(("parallel",)),
        name="dense_ffn",
    )(x2d, g, wg, wu, wd)


def _rope64(y, cos, sin_a, sin_b):
    half = DIFF_HEAD_DIM // 2
    return (y * cos + pltpu.roll(y, LANES - half, 1) * sin_a
            + pltpu.roll(y, half, 1) * sin_b)


def _kvq_proj_kernel(x_ref, gkv_ref, gq_ref, wk_ref, wvt_ref, wqt_ref, cos_ref, sa_ref, sb_ref,
                     cost_ref, sint_ref, k_ref, vt_ref, qt_ref):
    d = D_MODEL
    xn = _rms(x_ref[...], NORM_EPS)
    hkv = (xn * gkv_ref[...]).astype(BF16)
    hq = (xn * gq_ref[...]).astype(BF16)
    cos = cos_ref[...]
    sa = sa_ref[...]
    sb = sb_ref[...]
    yk = _dot(hkv, wk_ref[...])
    for j in range(d // LANES):
        cols = slice(j * LANES, (j + 1) * LANES)
        k_ref[:, cols] = _rope64(yk[:, cols], cos, sa, sb).astype(BF16)

    yvt = _dot_nt(wvt_ref[...], hkv)
    ones_row = (lax.broadcasted_iota(jnp.int32, (V_ROWS - DIFF_V_DIM, yvt.shape[1]), 0) == 0
                ).astype(BF16)
    for h in range(DIFF_HEADS):
        vt_ref[h, 0:DIFF_V_DIM, :] = yvt[h * DIFF_V_DIM:(h + 1) * DIFF_V_DIM, :].astype(BF16)
        vt_ref[h, DIFF_V_DIM:V_ROWS, :] = ones_row

    yqt = _dot_nt(wqt_ref[...], hq)
    ct = cost_ref[...]
    st = sint_ref[...]
    half = DIFF_HEAD_DIM // 2
    q_scale = DIFF_HEAD_DIM ** -0.5 * LOG2E
    for g in range(d // DIFF_HEAD_DIM):
        r0 = g * DIFF_HEAD_DIM
        a = yqt[r0:r0 + half, :]
        b = yqt[r0 + half:r0 + 2 * half, :]
        qt_ref[r0:r0 + half, :] = ((a * ct - b * st) * q_scale).astype(BF16)
        qt_ref[r0 + half:r0 + 2 * half, :] = ((b * ct + a * st) * q_scale).astype(BF16)


def _kvq_proj(x, gkv, gq, wk, wvt, wqt, cos, sin_a, sin_b, cos_t, sin_t):
    b, s, d = x.shape
    tm = min(TOKEN_TILE, s)
    n_seq = s // tm
    tk = min(ATTN_TILE, s)
    per_blk = tk // tm
    half = DIFF_HEAD_DIM // 2
    pos = lambda bi, i: (i, 0)
    pos_t = lambda bi, i: (0, i)
    return pl.pallas_call(
        _kvq_proj_kernel,
        grid=(b, n_seq),
        in_specs=[
            pl.BlockSpec((None, tm, d), lambda bi, i: (bi, i, 0)),
            _const_spec((1, d)),
            _const_spec((1, d)),
            _const_spec((d, d)),
            _const_spec((d, d)),
            _const_spec((d, d)),
            pl.BlockSpec((tm, LANES), pos),
            pl.BlockSpec((tm, LANES), pos),
            pl.BlockSpec((tm, LANES), pos),
            pl.BlockSpec((half, tm), pos_t),
            pl.BlockSpec((half, tm), pos_t),
        ],
        out_specs=[
            pl.BlockSpec((None, tm, d), lambda bi, i: (bi, i, 0)),
            pl.BlockSpec((None, DIFF_HEADS, None, V_ROWS, tm),
                         lambda bi, i: (bi, 0, i // per_blk, 0, i % per_blk)),
            pl.BlockSpec((None, d, tm), lambda bi, i: (bi, 0, i)),
        ],
        out_shape=[
            jax.ShapeDtypeStruct((b, s, d), BF16),
            jax.ShapeDtypeStruct((b, DIFF_HEADS, s // tk, V_ROWS, tk), BF16),
            jax.ShapeDtypeStruct((b, d, s), BF16),
        ],
        compiler_params=_params(("parallel", "parallel")),
        name="kvq_proj",
    )(x, gkv, gq, wk, wvt, wqt, cos, sin_a, sin_b, cos_t, sin_t)


MXU_COLS = 512
MASKED_LOGIT = -1e30
LOGITS_AHEAD = 1


def _attn_kernel(lam_ref, qt_ref, k_ref, vt_ref, g_ref, o_ref, qq_ref, m_ref, acc_ref,
                 s_ref, *, out_scale):
    tq = qt_ref.shape[1]
    tk = vt_ref.shape[2]
    assert tq == tk
    r = 2 * tq
    qi = pl.program_id(2)

    qt = qt_ref[...]
    feat = lax.broadcasted_iota(jnp.int32, qt.shape, 0)
    zero = jnp.zeros_like(qt)
    qq_ref[:, 0:tq] = jnp.where(feat < DIFF_HEAD_DIM, qt, zero)
    qq_ref[:, tq:r] = jnp.where(feat >= DIFF_HEAD_DIM, qt, zero)
    m_ref[...] = jnp.full_like(m_ref, MASKED_LOGIT)
    acc_ref[...] = jnp.zeros_like(acc_ref)

    n_groups = r // MXU_COLS
    ahead = s_ref.shape[0]
    assert n_groups >= ahead

    def keys_needed(c):
        return min(tk, (c * MXU_COLS) % tq + MXU_COLS)

    def logits(j, c, n_keys=tk):
        start = pl.multiple_of(j * tk, tk)
        return _dot(k_ref[pl.ds(start, n_keys), :],
                    qq_ref[:, c * MXU_COLS:(c + 1) * MXU_COLS])

    def block(j, is_diagonal):
        vt = vt_ref[j]
        pending = [s_ref[a] for a in range(ahead)]
        for c in range(n_groups):
            cols = slice(c * MXU_COLS, (c + 1) * MXU_COLS)
            s = pending.pop(0)
            if c + ahead < n_groups:
                n_next = keys_needed(c + ahead) if is_diagonal else tk
                pending.append(logits(j, c + ahead, n_next))
            elif not is_diagonal:
                s_ref[c + ahead - n_groups] = logits(j + 1, c + ahead - n_groups)
            vt_c = vt
            if is_diagonal:
                n_keys = keys_needed(c)
                s = s[0:n_keys, :]
                vt_c = vt[:, 0:n_keys]
                key = lax.broadcasted_iota(jnp.int32, s.shape, 0)
                qry = lax.broadcasted_iota(jnp.int32, s.shape, 1) + (c * MXU_COLS) % tq
                s = jnp.where(key <= qry, s, MASKED_LOGIT)
            m_prev = m_ref[:, cols]
            m_new = jnp.maximum(m_prev, jnp.max(s, axis=0, keepdims=True))
            alpha = jnp.exp2(m_prev - m_new)
            p = jnp.exp2((s - m_new).astype(BF16))
            acc_ref[:, cols] = alpha * acc_ref[:, cols] + _dot(vt_c, p)
            m_ref[:, cols] = m_new

    def body(j, carry):
        block(j, False)
        return carry

    for a in range(ahead):
        s_ref[a] = logits(0, a)
    lax.fori_loop(0, qi, body, 0)
    block(qi, True)

    o = acc_ref[0:DIFF_V_DIM, :] * (1.0 / acc_ref[DIFF_V_DIM:DIFF_V_DIM + 1, :])
    out = o[:, 0:tq] - lam_ref[0] * o[:, tq:r]
    ms = jnp.mean(out * out, axis=0, keepdims=True)
    out = out * lax.rsqrt(ms + SUBLN_EPS) * g_ref[...] * out_scale
    o_ref[...] = out.T.astype(BF16)


def _diff_attention(lam, qt, k, vt, subln_g_cols, lambda_init):
    b, s, d = k.shape
    tq = vt.shape[-1]
    n_blk = s // tq
    return pl.pallas_call(
        functools.partial(_attn_kernel, out_scale=1.0 - lambda_init),
        grid=(b, DIFF_HEADS, n_blk),
        in_specs=[
            pl.BlockSpec(memory_space=pltpu.SMEM),
            pl.BlockSpec((None, LANES, tq), lambda bi, h, i: (bi, h, i)),
            pl.BlockSpec((None, s, LANES), lambda bi, h, i: (bi, 0, h)),
            pl.BlockSpec((None, None, n_blk, V_ROWS, tq), lambda bi, h, i: (bi, h, 0, 0, 0)),
            _const_spec((DIFF_V_DIM, tq)),
        ],
        out_specs=pl.BlockSpec((None, tq, LANES), lambda bi, h, i: (bi, i, h)),
        out_shape=jax.ShapeDtypeStruct((b, s, d), BF16),
        scratch_shapes=[
            pltpu.VMEM((LANES, 2 * tq), BF16),
            pltpu.VMEM((1, 2 * tq), F32),
            pltpu.VMEM((V_ROWS, 2 * tq), F32),
            pltpu.VMEM((LOGITS_AHEAD, tq, MXU_COLS), F32),
        ],
        compiler_params=_params(("parallel", "parallel", "arbitrary")),
        name="diff_attention",
    )(lam, qt, k, vt, subln_g_cols)


def _split_bf16(x):
    hi = x.astype(BF16)
    lo = (x - hi.astype(F32)).astype(BF16)
    return hi, lo


ROUTE_ROWS = 8


def _oproj_router_kernel(a_ref, x_ref, wo_ref, g_ref, wrt_ref, x_out_ref, h_ref, route_ref):
    x = x_ref[...] + _dot(a_ref[...], wo_ref[...])
    x_out_ref[...] = x
    h = _rms(x, NORM_EPS) * g_ref[...]
    h_hi, h_lo = _split_bf16(h)
    h_ref[...] = h_hi
    wrt = wrt_ref[...]
    a = _dot_nt(wrt, h_hi)
    b = _dot_nt(wrt, h_lo)
    ne = N_EXPERTS
    logits = a[0:ne, :] + (b[0:ne, :] + a[ne:2 * ne, :])
    expert = lax.broadcasted_iota(jnp.int32, logits.shape, 0).astype(F32)
    neg = -jnp.inf
    m1 = jnp.max(logits, axis=0, keepdims=True)
    i1 = jnp.min(jnp.where(logits == m1, expert, float(ne)), axis=0, keepdims=True)
    rest = jnp.where(expert == i1, neg, logits)
    m2 = jnp.max(rest, axis=0, keepdims=True)
    i2 = jnp.min(jnp.where(rest == m2, expert, float(ne)), axis=0, keepdims=True)
    e = jnp.exp(m2 - m1)
    w1 = 1.0 / (1.0 + e)
    w2 = e / (1.0 + e)
    rec = lax.broadcasted_iota(jnp.int32, (ROUTE_ROWS, logits.shape[1]), 0)
    route_ref[...] = jnp.where(rec == 0, i1, jnp.where(rec == 1, i2,
                               jnp.where(rec == 2, w1, jnp.where(rec == 3, w2, 0.0))))


def _oproj_router(attn2d, x2d, wo, g, wrt):
    t, d = x2d.shape
    tm = min(TOKEN_TILE, t)
    tok = lambda i: (i, 0)
    return pl.pallas_call(
        _oproj_router_kernel,
        grid=(t // tm,),
        in_specs=[
            pl.BlockSpec((tm, d), tok),
            pl.BlockSpec((tm, d), tok),
            _const_spec((d, d)),
            _const_spec((1, d)),
            _const_spec((2 * N_EXPERTS, d)),
        ],
        out_specs=[
            pl.BlockSpec((tm, d), tok),
            pl.BlockSpec((tm, d), tok),
            pl.BlockSpec((ROUTE_ROWS, tm), lambda i: (0, i)),
        ],
        out_shape=[
            jax.ShapeDtypeStruct((t, d), F32),
            jax.ShapeDtypeStruct((t, d), BF16),
            jax.ShapeDtypeStruct((ROUTE_ROWS, t), F32),
        ],
        compiler_params=_params(("parallel",)),
        name="oproj_router",
    )(attn2d, x2d, wo, g, wrt)


WEIGHT_SLOTS = 2


def _expert_kernel(te_ref, nt_ref, xs_ref, rw_ref, wg_hbm, wu_hbm, wd_hbm, y_prev_hbm, y_ref,
                   wg_ref, wu_ref, wd_ref, sg_ref, su_ref, sd_ref, sem, *, fc):
    del y_prev_hbm
    i = pl.program_id(0)
    e = te_ref[i]
    active = i < nt_ref[0]
    first = jnp.logical_and(active, jnp.logical_or(i == 0, e != te_ref[jnp.maximum(i - 1, 0)]))
    n_chunks = wg_ref.shape[1] // fc

    def chunk_copies(c):
        slot = c % WEIGHT_SLOTS
        cols = pl.ds(c * fc, fc)
        return (
            pltpu.make_async_copy(wg_hbm.at[e, :, cols], sg_ref.at[slot], sem.at[0, slot]),
            pltpu.make_async_copy(wu_hbm.at[e, :, cols], su_ref.at[slot], sem.at[1, slot]),
            pltpu.make_async_copy(wd_hbm.at[e, cols, :], sd_ref.at[slot], sem.at[2, slot]),
        )

    def swiglu_chunk(xs, c):
        lo, hi = c * fc, (c + 1) * fc
        a = _dot(xs, wg_ref[:, lo:hi])
        u = _dot(xs, wu_ref[:, lo:hi])
        return _dot((_silu(a) * u).astype(BF16), wd_ref[lo:hi, :])

    def finish(acc):
        y_ref[...] = (acc * rw_ref[...]).astype(y_ref.dtype)

    @pl.when(first)
    def _():
        for c in range(min(WEIGHT_SLOTS, n_chunks)):
            for cp in chunk_copies(c):
                cp.start()
        xs = xs_ref[...]
        acc = jnp.zeros(y_ref.shape, F32)
        for c in range(n_chunks):
            slot = c % WEIGHT_SLOTS
            lo, hi = c * fc, (c + 1) * fc
            for cp in chunk_copies(c):
                cp.wait()
            wg_ref[:, lo:hi] = sg_ref[slot].astype(BF16)
            wu_ref[:, lo:hi] = su_ref[slot].astype(BF16)
            wd_ref[lo:hi, :] = sd_ref[slot].astype(BF16)
            if c + WEIGHT_SLOTS < n_chunks:
                for cp in chunk_copies(c + WEIGHT_SLOTS):
                    cp.start()
            acc = acc + swiglu_chunk(xs, c)
        finish(acc)

    @pl.when(jnp.logical_and(active, jnp.logical_not(first)))
    def _():
        xs = xs_ref[...]
        acc = jnp.zeros(y_ref.shape, F32)
        for c in range(n_chunks):
            acc = acc + swiglu_chunk(xs, c)
        finish(acc)

    @pl.when(jnp.logical_not(active))
    def _():
        y_ref[...] = jnp.zeros_like(y_ref)


def _experts(tile_expert, n_tiles_used, xs, row_w, wg, wu, wd, y_prev, *, tile_lo):
    p, d = xs.shape
    f = wg.shape[2]
    tm = EXPERT_TILE
    fc = EXPERT_F_CHUNK
    assert f % fc == 0
    n_tiles = p // tm

    def row(i, te, nt):
        return (jnp.maximum(jnp.minimum(i, nt[0] - 1), 0), 0)

    operands = (tile_expert, n_tiles_used, xs, row_w, wg, wu, wd, y_prev)
    grid_spec = pltpu.PrefetchScalarGridSpec(
        num_scalar_prefetch=2,
        grid=(n_tiles,),
        in_specs=[
            pl.BlockSpec((tm, d), row),
            pl.BlockSpec((tm, 1), row),
            pl.BlockSpec(memory_space=pl.ANY),
            pl.BlockSpec(memory_space=pl.ANY),
            pl.BlockSpec(memory_space=pl.ANY),
            pl.BlockSpec(memory_space=pl.ANY),
        ],
        out_specs=pl.BlockSpec((tm, d), lambda i, te, nt: (i + tile_lo, 0)),
        scratch_shapes=[
            pltpu.VMEM((d, f), BF16),
            pltpu.VMEM((d, f), BF16),
            pltpu.VMEM((f, d), BF16),
            pltpu.VMEM((WEIGHT_SLOTS, d, fc), F32),
            pltpu.VMEM((WEIGHT_SLOTS, d, fc), F32),
            pltpu.VMEM((WEIGHT_SLOTS, fc, d), F32),
            pltpu.SemaphoreType.DMA((3, WEIGHT_SLOTS)),
        ],
    )
    return pl.pallas_call(
        functools.partial(_expert_kernel, fc=fc),
        grid_spec=grid_spec,
        out_shape=jax.ShapeDtypeStruct(y_prev.shape, y_prev.dtype),
        input_output_aliases={len(operands) - 1: 0},
        compiler_params=_params(("arbitrary",)),
        name="experts",
    )(*operands)


def _route_tables(route, tm):
    t = route.shape[1]
    n_pairs = t * TOP_K
    n_tiles = n_pairs // tm + N_EXPERTS - 1
    p = n_tiles * tm
    e_pair = route[0:TOP_K, :].astype(jnp.int32).reshape(n_pairs)
    w_pair = route[TOP_K:2 * TOP_K, :].reshape(n_pairs)
    onehot = (e_pair[None, :] == jnp.arange(N_EXPERTS, dtype=jnp.int32)[:, None]).astype(jnp.int32)
    counts = jnp.sum(onehot, axis=1)
    rank = jnp.sum((jnp.cumsum(onehot, axis=1) - onehot) * onehot, axis=0)
    tiles_per = (counts + tm - 1) // tm
    tile_end = jnp.cumsum(tiles_per)
    tile_start = tile_end - tiles_per
    dest = tile_start[e_pair] * tm + rank
    tile_ids = jnp.arange(n_tiles, dtype=jnp.int32)
    tile_expert = jnp.minimum(
        jnp.sum((tile_ids[:, None] >= tile_end[None, :]).astype(jnp.int32), axis=1),
        N_EXPERTS - 1)
    n_used = tile_end[-1:].astype(jnp.int32)
    pair_ids = jnp.arange(n_pairs, dtype=jnp.int32)
    by_expert = jnp.sort(e_pair * n_pairs + pair_ids) % n_pairs
    group_first = jnp.cumsum(counts) - counts
    row_in_group = tile_ids * tm - tile_start[tile_expert] * tm
    slot = (group_first[tile_expert] + row_in_group)[:, None] + jnp.arange(tm, dtype=jnp.int32)[None, :]
    valid = (row_in_group[:, None] + jnp.arange(tm, dtype=jnp.int32)[None, :]
             < counts[tile_expert][:, None])
    row_pair = jnp.where(valid, by_expert[jnp.clip(slot, 0, n_pairs - 1)], n_pairs).reshape(p)
    row_token = jnp.where(row_pair < n_pairs, row_pair % t, 0)
    row_w = jnp.concatenate([w_pair, jnp.zeros((1,), F32)])[row_pair]
    return row_token, row_w.reshape(p, 1), tile_expert, n_used, dest


def _final_kernel(x_ref, ya_ref, yb_ref, g_ref, o_ref):
    x = x_ref[...] + (ya_ref[...].astype(F32) + yb_ref[...].astype(F32))
    o_ref[...] = _rms(x, NORM_EPS) * g_ref[...]


def _final(x2d, y_picks, g):
    t, d = x2d.shape
    tm = min(2 * TOKEN_TILE, t)
    n_steps = t // tm
    tok = lambda i: (i, 0)
    return pl.pallas_call(
        _final_kernel,
        grid=(n_steps,),
        in_specs=[
            pl.BlockSpec((tm, d), tok),
            pl.BlockSpec((tm, d), tok),
            pl.BlockSpec((tm, d), lambda i: (i + n_steps, 0)),
            _const_spec((1, d)),
        ],
        out_specs=pl.BlockSpec((tm, d), tok),
        out_shape=jax.ShapeDtypeStruct((t, d), F32),
        compiler_params=_params(("parallel",)),
        name="final_norm",
    )(x2d, y_picks, y_picks, g)


def _rope_angles(seq, dim):
    inv = 1.0 / (ROPE_THETA ** (jnp.arange(0, dim, 2, dtype=F32) / dim))
    return jnp.arange(seq, dtype=F32)[:, None] * inv[None, :]


def _retention_decays():
    h, c = RET_HEADS, RET_CHUNK
    log_gamma = jnp.log1p(-jnp.exp2(-5.0 - jnp.arange(h, dtype=F32)))
    pos = jnp.arange(c, dtype=F32)
    rel = pos[:, None] - pos[None, :]
    intra = jnp.where(rel >= 0, jnp.exp(log_gamma[:, None, None] * jnp.maximum(rel, 0.0)), 0.0)
    q_decay = jnp.exp(log_gamma[:, None] * (pos + 1.0))[:, :, None]
    k_decay = jnp.exp(log_gamma[:, None] * (c - 1.0 - pos))[:, :, None]
    chunk_decay = jnp.exp(log_gamma * c)
    return chunk_decay, intra, q_decay, k_decay


def kernel(x, attn_norm_g, ffn_norm_g, ret_w_in, ret_w_out, kv_norm_g, w_kv, diff_w_q,
           diff_lam_q1, diff_lam_k1, diff_lam_q2, diff_lam_k2, diff_subln_g, diff_w_o,
           ffn_w_gate, ffn_w_up, ffn_w_down, moe_w_router, moe_w_gate, moe_w_up, moe_w_down,
           final_norm_g):
    b, s, d = x.shape
    t = b * s
    row = lambda g: g.reshape(1, -1).astype(F32)
    bf = lambda w: w.astype(BF16)

    ang = _rope_angles(s, RET_QK_DIM)
    q, k, v, gate = _ret_proj(x.reshape(t, d), row(attn_norm_g[0]), bf(ret_w_in[0]),
                              jnp.cos(ang), jnp.sin(ang), s)
    chunk_decay, intra_decay, q_decay, k_decay = _retention_decays()
    x1 = _retention(chunk_decay, q.reshape(b, s, d), k.reshape(b, s, d),
                    v.reshape(b, s, 2 * d), gate.reshape(b, s, 2 * d), x,
                    bf(ret_w_out[0]), intra_decay, q_decay, k_decay)
    x2 = _ffn(x1.reshape(t, d), row(ffn_norm_g[0]), bf(ffn_w_gate[0]), bf(ffn_w_up[0]),
              bf(ffn_w_down[0]))

    ang = _rope_angles(s, DIFF_HEAD_DIM)
    cos32, sin32 = lax.optimization_barrier((jnp.cos(ang), jnp.sin(ang)))
    zero32 = jnp.zeros_like(sin32)
    pairs = LANES // DIFF_HEAD_DIM
    cos = jnp.concatenate([cos32, cos32] * pairs, axis=-1)
    sin_a = jnp.concatenate([-sin32, zero32] * pairs, axis=-1)
    sin_b = jnp.concatenate([zero32, sin32] * pairs, axis=-1)
    ks, vt, qt = _kvq_proj(x2.reshape(b, s, d), row(kv_norm_g), row(attn_norm_g[1]),
                           bf(w_kv[:, :d]), bf(w_kv[:, d:].T), bf(diff_w_q[0].T),
                           cos, sin_a, sin_b, cos32.T, sin32.T)
    lambda_init = 0.8 - 0.6 * math.exp(-0.3 * 1)
    lam = (jnp.exp(jnp.sum(diff_lam_q1[0].astype(F32) * diff_lam_k1[0].astype(F32)))
           - jnp.exp(jnp.sum(diff_lam_q2[0].astype(F32) * diff_lam_k2[0].astype(F32)))
           + lambda_init).reshape(1)
    g_cols = jnp.broadcast_to(diff_subln_g[0].astype(F32)[:, None], (DIFF_V_DIM, vt.shape[-1]))
    attn = _diff_attention(lam, qt, ks, vt, g_cols, lambda_init)

    wr_t = moe_w_router[0].astype(F32).T
    wr_hi = wr_t.astype(BF16)
    wr_lo = (wr_t - wr_hi.astype(F32)).astype(BF16)
    x3, h, route = _oproj_router(attn.reshape(t, d), x2, bf(diff_w_o[0]), row(ffn_norm_g[1]),
                                 jnp.concatenate([wr_hi, wr_lo], axis=0))
    row_token, row_w, tile_expert, n_used, dest = _route_tables(route, EXPERT_TILE)
    tm = EXPERT_TILE
    n_tiles = tile_expert.shape[0]
    n_head = n_tiles // 4
    y = jnp.zeros((n_tiles * tm, d), BF16)
    for lo, hi in ((0, n_head), (n_head, n_tiles)):
        xs = jnp.take(h, row_token[lo * tm:hi * tm], axis=0, mode="clip")
        y = _experts(tile_expert[lo:hi], jnp.clip(n_used - lo, 0, hi - lo), xs,
                     row_w[lo * tm:hi * tm], moe_w_gate[0], moe_w_up[0], moe_w_down[0], y,
                     tile_lo=lo)
    y_picks = jnp.take(y, dest, axis=0, mode="clip")
    out = _final(x3, y_picks, row(final_norm_g))
    return out.reshape(b, s, d)
```
